```python
import jax, jax.numpy as jnp
from jax import lax
import numpy as np

D_MODEL = 1024
BATCH = 4
SEQ = 4096
DEPTH = 4

N_META = 16
SB_HEADS = 8
SB_HEAD_DIM = 64
SB_WIDTH = SB_HEADS * SB_HEAD_DIM
SB_BLOCK = 128
GDN_HEADS = 4
GDN_HEAD_DIM = 128
GDN_WIDTH = GDN_HEADS * GDN_HEAD_DIM
GDN_CHUNK = 64
CONV_WIDTH = 4
MIX_WIDTH = SB_WIDTH + GDN_WIDTH
IN_WIDTH = 3 * SB_WIDTH + 4 * GDN_WIDTH + 2 * GDN_HEADS
N_GROUPS = 4
EXPERTS_PER_GROUP = 8
N_EXPERTS = N_GROUPS * EXPERTS_PER_GROUP
TOP_K = 2
D_EXPERT = 256
LN_EPS = 1e-5
RMS_EPS = 1e-6
DEEPNORM_ALPHA = float((2 * DEPTH) ** 0.25)
DEEPNORM_BETA = float((8 * DEPTH) ** -0.25)

kernel_name = "hymba_stickbreak_gdn_hmoe_deepnorm"


def layer_norm(x, g, b):
    xf = x.astype(jnp.float32)
    mu = jnp.mean(xf, axis=-1, keepdims=True)
    var = jnp.mean(jnp.square(xf - mu), axis=-1, keepdims=True)
    y = (xf - mu) * lax.rsqrt(var + LN_EPS) * g.astype(jnp.float32) + b.astype(jnp.float32)
    return y.astype(x.dtype)


def rms_norm(x, g):
    xf = x.astype(jnp.float32)
    return xf * lax.rsqrt(jnp.mean(jnp.square(xf), axis=-1, keepdims=True) + RMS_EPS) * g.astype(jnp.float32)


def l2_normalize(x):
    return x * lax.rsqrt(jnp.sum(jnp.square(x), axis=-1, keepdims=True) + RMS_EPS)


def causal_depthwise_conv(x, w):
    ch = x.shape[-1]
    return lax.conv_general_dilated(
        x, w[:, None, :].astype(x.dtype), window_strides=(1,),
        padding=[(CONV_WIDTH - 1, 0)], dimension_numbers=("NWC", "WIO", "NWC"),
        feature_group_count=ch)


def stick_breaking_attention(q, k, v):
    lp, dh = q.shape[2], q.shape[3]
    scale = dh ** -0.5
    outs = []
    for start in range(0, lp, SB_BLOCK):
        stop = start + SB_BLOCK
        qb = q[:, :, start:stop].astype(jnp.float32)
        kp = k[:, :, :stop].astype(jnp.float32)
        vp = v[:, :, :stop].astype(jnp.float32)
        z = jnp.einsum("bhqd,bhkd->bhqk", qb, kp) * scale
        visible = jnp.arange(stop)[None, :] < jnp.arange(start, stop)[:, None]
        log_not_beta = jnp.where(visible, -jax.nn.softplus(z), 0.0)
        between = lax.cumsum(log_not_beta, axis=3, reverse=True) - log_not_beta
        weights = jnp.where(visible, jnp.exp(jax.nn.log_sigmoid(z) + between), 0.0)
        outs.append(jnp.einsum("bhqk,bhkd->bhqd", weights, vp))
    return jnp.concatenate(outs, axis=2)


def gated_delta_rule_chunked(q, k, v, beta, g):
    c, dk, dv = q.shape[3], q.shape[4], v.shape[4]
    g_cum = jnp.cumsum(g, axis=-1)
    lower_incl = jnp.tril(jnp.ones((c, c), dtype=bool))
    lower_strict = jnp.tril(jnp.ones((c, c), dtype=bool), -1)
    diff = g_cum[..., :, None] - g_cum[..., None, :]
    decay_mask = jnp.where(lower_incl, jnp.exp(jnp.where(lower_incl, diff, 0.0)), 0.0)
    k_beta = k * beta[..., None]
    v_beta = v * beta[..., None]
    a_strict = jnp.where(lower_strict, jnp.einsum("bhnid,bhnjd->bhnij", k_beta, k) * decay_mask, 0.0)
    rhs = jnp.concatenate([k_beta * jnp.exp(g_cum)[..., None], v_beta], axis=-1)
    sol = lax.linalg.triangular_solve(jnp.eye(c, dtype=jnp.float32) + a_strict, rhs,
                                      left_side=True, lower=True)
    w_cd, u = sol[..., :dk], sol[..., dk:]
    qk_intra = jnp.einsum("bhnid,bhnjd->bhnij", q, k) * decay_mask
    q_dec = q * jnp.exp(g_cum)[..., None]
    k_to_end = k * jnp.exp(g_cum[..., -1:] - g_cum)[..., None]
    chunk_decay = jnp.exp(g_cum[..., -1])

    def step(state, inp):
        w_n, u_n, qd_n, qk_n, ke_n, cd_n = inp
        v_new = u_n - jnp.einsum("bhcd,bhde->bhce", w_n, state)
        o = jnp.einsum("bhcd,bhde->bhce", qd_n, state) + jnp.einsum("bhij,bhje->bhie", qk_n, v_new)
        state = state * cd_n[..., None, None] + jnp.einsum("bhcd,bhce->bhde", ke_n, v_new)
        return state, o

    xs = tuple(jnp.moveaxis(a, 2, 0) for a in (w_cd, u, q_dec, qk_intra, k_to_end, chunk_decay))
    init = jnp.zeros(q.shape[:2] + (dk, dv), jnp.float32)
    _, o = lax.scan(step, init, xs)
    return jnp.moveaxis(o, 0, 2)


def hybrid_mixer(x, w_in, conv_w, a_log, dt_bias, sb_norm_g, gdn_norm_g, w_out):
    b, l, _ = x.shape
    proj = x @ w_in
    s1 = 3 * SB_WIDTH
    s2 = s1 + 3 * GDN_WIDTH
    s3 = s2 + GDN_WIDTH
    s4 = s3 + GDN_HEADS
    sb_qkv, gdn_qkv, gdn_z, gdn_b, gdn_a = jnp.split(proj, [s1, s2, s3, s4], axis=-1)

    pad_sb = (-l) % SB_BLOCK
    sb_qkv = jnp.pad(sb_qkv, ((0, 0), (0, pad_sb), (0, 0)))
    sb_qkv = sb_qkv.reshape(b, l + pad_sb, 3, SB_HEADS, SB_HEAD_DIM).transpose(2, 0, 3, 1, 4)
    o_sb = stick_breaking_attention(sb_qkv[0], sb_qkv[1], sb_qkv[2])[:, :, :l]
    o_sb = rms_norm(o_sb, sb_norm_g).transpose(0, 2, 1, 3).reshape(b, l, SB_WIDTH)

    gdn_qkv = jax.nn.silu(causal_depthwise_conv(gdn_qkv, conv_w)).astype(jnp.float32)
    gdn_qkv = gdn_qkv.reshape(b, l, 3, GDN_HEADS, GDN_HEAD_DIM)
    q = l2_normalize(gdn_qkv[:, :, 0]) * (GDN_HEAD_DIM ** -0.5)
    k = l2_normalize(gdn_qkv[:, :, 1])
    v = gdn_qkv[:, :, 2]
    beta = jax.nn.sigmoid(gdn_b.astype(jnp.float32))
    g = -jnp.exp(a_log.astype(jnp.float32)) * jax.nn.softplus(
        gdn_a.astype(jnp.float32) + dt_bias.astype(jnp.float32))
    front = (-N_META) % GDN_CHUNK
    back = (-(l + front)) % GDN_CHUNK
    lc = l + front + back
    n_chunks = lc // GDN_CHUNK

    def to_chunks(a):
        a = jnp.pad(a, ((0, 0), (front, back)) + ((0, 0),) * (a.ndim - 2))
        a = a.reshape((b, n_chunks, GDN_CHUNK) + a.shape[2:])
        return jnp.moveaxis(a, 3, 1)

    o_gdn = gated_delta_rule_chunked(to_chunks(q), to_chunks(k), to_chunks(v),
                                     to_chunks(beta), to_chunks(g))
    o_gdn = o_gdn.transpose(0, 2, 3, 1, 4).reshape(b, lc, GDN_HEADS, GDN_HEAD_DIM)[:, front:front + l]
    z = gdn_z.reshape(b, l, GDN_HEADS, GDN_HEAD_DIM).astype(jnp.float32)
    o_gdn = (rms_norm(o_gdn, gdn_norm_g) * jax.nn.silu(z)).reshape(b, l, GDN_WIDTH)

    mixed = jnp.concatenate([o_sb, o_gdn], axis=-1)
    return (mixed @ w_out.astype(jnp.float32)).astype(x.dtype)


def hierarchical_moe(x, w_group, b_group, w_expert, b_expert, w1, w3, w2):
    t = x.shape[0]
    group_prob = jax.nn.softmax((x @ w_group + b_group).astype(jnp.float32), axis=-1)
    g_val, g_idx = lax.top_k(group_prob, 1)
    group_onehot = jax.nn.one_hot(g_idx[:, 0], N_GROUPS, dtype=jnp.float32)
    expert_logits = (x @ w_expert + b_expert).astype(jnp.float32).reshape(t, N_GROUPS, EXPERTS_PER_GROUP)
    in_group = jnp.einsum("tg,tge->te", group_onehot, expert_logits)
    e_val, e_idx = lax.top_k(in_group, TOP_K)
    e_w = jax.nn.softmax(e_val, axis=-1)
    within = jnp.einsum("tk,tke->te", e_w, jax.nn.one_hot(e_idx, EXPERTS_PER_GROUP, dtype=jnp.float32))
    combine = group_onehot[:, :, None] * (g_val * within)[:, None, :]
    y = jnp.zeros((t, x.shape[1]), jnp.float32)
    for gi in range(N_GROUPS):
        sl = slice(gi * EXPERTS_PER_GROUP, (gi + 1) * EXPERTS_PER_GROUP)
        h = jax.nn.silu(jnp.einsum("td,edf->tef", x, w1[sl]).astype(jnp.float32)) * \
            jnp.einsum("td,edf->tef", x, w3[sl]).astype(jnp.float32)
        y = y + jnp.einsum("tef,efd->td", h * combine[:, gi, :, None], w2[sl].astype(jnp.float32))
    return y.astype(x.dtype)


def setup_inputs(seed: int = 0) -> dict:
    key = jax.random.key(seed)
    ks = jax.random.split(key, 24)
    f32 = jnp.float32
    nrm = lambda k, shape, s: jax.random.normal(k, shape, f32) * s
    dt = jnp.exp(jax.random.uniform(ks[6], (DEPTH, GDN_HEADS), f32, np.log(1e-3), np.log(1e-1)))
    return {
        "x": jax.random.normal(ks[0], (BATCH, SEQ, D_MODEL), f32),
        "meta_tokens": nrm(ks[1], (N_META, D_MODEL), 1.0),
        "ln_in_g": 1.0 + nrm(ks[2], (D_MODEL,), 0.02),
        "ln_in_b": nrm(ks[3], (D_MODEL,), 0.02),
        "w_in": nrm(ks[4], (DEPTH, D_MODEL, IN_WIDTH), D_MODEL ** -0.5),
        "conv_w": nrm(ks[5], (DEPTH, CONV_WIDTH, 3 * GDN_WIDTH), CONV_WIDTH ** -0.5),
        "a_log": jnp.log(jax.random.uniform(ks[7], (DEPTH, GDN_HEADS), f32, 1.0, 16.0)),
        "dt_bias": dt + jnp.log(-jnp.expm1(-dt)),
        "sb_norm_g": 1.0 + nrm(ks[8], (DEPTH, SB_HEAD_DIM), 0.02),
        "gdn_norm_g": 1.0 + nrm(ks[9], (DEPTH, GDN_HEAD_DIM), 0.02),
        "w_out": nrm(ks[10], (DEPTH, MIX_WIDTH, D_MODEL), MIX_WIDTH ** -0.5 * DEEPNORM_BETA),
        "ln1_g": 1.0 + nrm(ks[11], (DEPTH, D_MODEL), 0.02),
        "ln1_b": nrm(ks[12], (DEPTH, D_MODEL), 0.02),
        "w_group": nrm(ks[13], (DEPTH, D_MODEL, N_GROUPS), D_MODEL ** -0.5),
        "b_group": nrm(ks[14], (DEPTH, N_GROUPS), 0.01),
        "w_expert": nrm(ks[15], (DEPTH, D_MODEL, N_EXPERTS), D_MODEL ** -0.5),
        "b_expert": nrm(ks[16], (DEPTH, N_EXPERTS), 0.01),
        "w1": nrm(ks[17], (DEPTH, N_EXPERTS, D_MODEL, D_EXPERT), D_MODEL ** -0.5),
        "w3": nrm(ks[18], (DEPTH, N_EXPERTS, D_MODEL, D_EXPERT), D_MODEL ** -0.5),
        "w2": nrm(ks[19], (DEPTH, N_EXPERTS, D_EXPERT, D_MODEL), D_EXPERT ** -0.5 * DEEPNORM_BETA),
        "ln2_g": 1.0 + nrm(ks[20], (DEPTH, D_MODEL), 0.02),
        "ln2_b": nrm(ks[21], (DEPTH, D_MODEL), 0.02),
    }


def reference(x, meta_tokens, ln_in_g, ln_in_b, w_in, conv_w, a_log, dt_bias, sb_norm_g,
              gdn_norm_g, w_out, ln1_g, ln1_b, w_group, b_group, w_expert, b_expert,
              w1, w3, w2, ln2_g, ln2_b):
    b = x.shape[0]
    meta = jnp.broadcast_to(meta_tokens.astype(x.dtype)[None], (b, N_META, D_MODEL))
    h = layer_norm(jnp.concatenate([meta, x], axis=1), ln_in_g, ln_in_b)
    l = h.shape[1]
    for i in range(DEPTH):
        mix = hybrid_mixer(h, w_in[i], conv_w[i], a_log[i], dt_bias[i], sb_norm_g[i],
                           gdn_norm_g[i], w_out[i])
        h = layer_norm(DEEPNORM_ALPHA * h + mix, ln1_g[i], ln1_b[i])
        ffn = hierarchical_moe(h.reshape(b * l, D_MODEL), w_group[i], b_group[i], w_expert[i],
                               b_expert[i], w1[i], w3[i], w2[i]).reshape(b, l, D_MODEL)
        h = layer_norm(DEEPNORM_ALPHA * h + ffn, ln2_g[i], ln2_b[i])
    return h[:, N_META:]
```

```python
import functools

import jax
import jax.numpy as jnp
from jax import lax
from jax.experimental import pallas as pl
from jax.experimental.pallas import tpu as pltpu

N_META = 16
SB_HEADS = 8
SB_HEAD_DIM = 64
SB_WIDTH = SB_HEADS * SB_HEAD_DIM
GDN_HEADS = 4
GDN_HEAD_DIM = 128
GDN_WIDTH = GDN_HEADS * GDN_HEAD_DIM
GDN_CHUNK = 64
CONV_WIDTH = 4
N_GROUPS = 4
EXPERTS_PER_GROUP = 8
N_EXPERTS = N_GROUPS * EXPERTS_PER_GROUP
D_EXPERT = 256
LN_EPS = 1e-5
RMS_EPS = 1e-6

LANES = 128
SB_BLOCK = 128
FRONT = (-N_META) % GDN_CHUNK
EXPERT_TILE = 256
VMEM_LIMIT = 56 * 1024 * 1024
EXP_UNDERFLOW = -88.0
NEG_BIG = -1e30

F32 = jnp.float32
BF16 = jnp.bfloat16


def _cparams(sem):
    return pltpu.CompilerParams(dimension_semantics=sem, vmem_limit_bytes=VMEM_LIMIT)


def _dot(a, b):
    return jnp.dot(a, b, preferred_element_type=F32)


def _dot_nt(a, b):
    return lax.dot_general(a, b, (((1,), (1,)), ((), ())), preferred_element_type=F32)


def _dot_tn(a, b):
    return lax.dot_general(a, b, (((0,), (0,)), ((), ())), preferred_element_type=F32)


def _split3(x):
    hi = x.astype(BF16)
    r1 = x - hi.astype(F32)
    mid = r1.astype(BF16)
    lo = (r1 - mid.astype(F32)).astype(BF16)
    return hi, mid, lo


def _dot_sel(m01, x):
    hi, mid, lo = _split3(x)
    return _dot(m01, hi) + _dot(m01, mid) + _dot(m01, lo)


def _silu(x):
    return x / (1.0 + jnp.exp(-x))


def _softplus(x):
    return jnp.maximum(x, 0.0) + jnp.log1p(jnp.exp(-jnp.abs(x)))


def _layer_norm(x, g, b):
    mu = jnp.mean(x, axis=-1, keepdims=True)
    xc = x - mu
    var = jnp.mean(xc * xc, axis=-1, keepdims=True)
    return xc * lax.rsqrt(var + LN_EPS) * g + b


def _row_valid(j, tm, l):
    r = j * tm + lax.broadcasted_iota(jnp.int32, (tm, 1), 0)
    return (r >= FRONT) & (r < FRONT + l)


def _ln_in_kernel(x_ref, g_ref, b_ref, o_ref, *, tm, l):
    y = _layer_norm(x_ref[...], g_ref[...], b_ref[...])
    o_ref[...] = jnp.where(_row_valid(pl.program_id(1), tm, l), y, 0.0)


def _ln_in(xp, g, b, *, tm, l):
    bsz, lp, d = xp.shape
    return pl.pallas_call(
        functools.partial(_ln_in_kernel, tm=tm, l=l),
        grid=(bsz, lp // tm),
        in_specs=[pl.BlockSpec((None, tm, d), lambda i, j: (i, j, 0)),
                  pl.BlockSpec((1, d), lambda i, j: (0, 0)),
                  pl.BlockSpec((1, d), lambda i, j: (0, 0))],
        out_specs=pl.BlockSpec((None, tm, d), lambda i, j: (i, j, 0)),
        out_shape=jax.ShapeDtypeStruct((bsz, lp, d), F32),
        compiler_params=_cparams(("arbitrary", "arbitrary")),
        name="ln_in",
    )(xp, g, b)


def _inproj_kernel(x_ref, w_ref, ws_ref, q_ref, k_ref, v_ref, g_ref, z_ref, gt_ref, wb_ref, wsb_ref):
    first = (pl.program_id(0) == 0) & (pl.program_id(1) == 0)

    @pl.when(first)
    def _():
        wb_ref[...] = w_ref[...].astype(BF16)
        wsb_ref[...] = ws_ref[...].astype(BF16)

    x = x_ref[...].astype(BF16)
    s1, s2, s3 = SB_WIDTH, 2 * SB_WIDTH, 3 * SB_WIDTH
    s4 = s3 + 3 * GDN_WIDTH
    s5 = s4 + GDN_WIDTH
    q_ref[...] = _dot(x, wb_ref[:, 0:s1]).astype(BF16)
    k_ref[...] = _dot(x, wb_ref[:, s1:s2]).astype(BF16)
    v_ref[...] = _dot(x, wb_ref[:, s2:s3]).astype(BF16)
    g_ref[...] = _dot(x, wb_ref[:, s3:s4])
    z_ref[...] = _dot(x, wb_ref[:, s4:s5])
    gt_ref[...] = _dot(x, wsb_ref[...])


def _inproj(h, w_main, w_small, *, tm):
    bsz, lp, d = h.shape
    n_main = w_main.shape[1]
    row = lambda w: pl.BlockSpec((None, tm, w), lambda i, j: (i, j, 0))
    shp = lambda w, dt: jax.ShapeDtypeStruct((bsz, lp, w), dt)
    return pl.pallas_call(
        _inproj_kernel,
        grid=(bsz, lp // tm),
        in_specs=[row(d),
                  pl.BlockSpec((d, n_main), lambda i, j: (0, 0)),
                  pl.BlockSpec((d, LANES), lambda i, j: (0, 0))],
        out_specs=[row(SB_WIDTH), row(SB_WIDTH), row(SB_WIDTH), row(3 * GDN_WIDTH), row(GDN_WIDTH), row(LANES)],
        out_shape=[shp(SB_WIDTH, BF16), shp(SB_WIDTH, BF16), shp(SB_WIDTH, BF16),
                   shp(3 * GDN_WIDTH, F32), shp(GDN_WIDTH, F32), shp(LANES, F32)],
        scratch_shapes=[pltpu.VMEM((d, n_main), BF16), pltpu.VMEM((d, LANES), BF16)],
        compiler_params=_cparams(("arbitrary", "arbitrary")),
        name="inproj",
    )(h, w_main, w_small)


def _sb_kernel(q_ref, k_ref, v_ref, g_ref, o_ref):
    blk = SB_BLOCK
    i = pl.program_id(2)
    q2 = q_ref[...] * jnp.asarray(SB_HEAD_DIM ** -0.5, BF16)
    lane = lax.broadcasted_iota(jnp.int32, (1, LANES), 1)
    row = lax.broadcasted_iota(jnp.int32, (blk, blk), 0)
    col = lax.broadcasted_iota(jnp.int32, (blk, blk), 1)
    ucat = jnp.concatenate([(row >= col).astype(BF16), jnp.ones((blk, blk), BF16)], axis=1)
    zero = jnp.zeros((), BF16)

    acc = jnp.zeros((blk, LANES), F32)
    for h in range(2):
        hm = (lane >= h * SB_HEAD_DIM) & (lane < (h + 1) * SB_HEAD_DIM)
        qh = jnp.where(hm, q2, zero)

        def body(carry, hm=hm, qh=qh):
            j, c, a, _ = carry
            start = pl.multiple_of(j * blk, blk)
            kj = k_ref[pl.ds(start, blk), :]
            vj = jnp.where(hm, v_ref[pl.ds(start, blk), :], zero)
            z = _dot_nt(qh, kj)
            vis = (j < i) | (col < row)
            lnb = jnp.where(vis, -_softplus(z), 0.0)
            hi = lnb.astype(BF16)
            lo = (lnb - hi.astype(F32)).astype(BF16)
            t = _dot(hi, ucat) + _dot(lo, ucat)
            between = c + t[:, :blk] - lnb
            p = jnp.where(vis, jnp.exp(z + lnb + between), 0.0)
            a = a + _dot(p.astype(BF16), vj)
            c = c + t[:, blk:]
            return j - 1, c, a, jnp.max(c)

        def cond(carry):
            j, _, _, cmax = carry
            return (j >= 0) & (cmax >= EXP_UNDERFLOW)

        _, _, acc, _ = lax.while_loop(cond, body, (i, jnp.zeros((blk, blk), F32), acc, jnp.float32(0.0)))

    sq = acc * acc
    ms = jnp.zeros_like(acc)
    for h in range(2):
        hm = (lane >= h * SB_HEAD_DIM) & (lane < (h + 1) * SB_HEAD_DIM)
        s = jnp.sum(jnp.where(hm, sq, 0.0), axis=-1, keepdims=True) * (1.0 / SB_HEAD_DIM)
        ms = jnp.where(hm, s, ms)
    o_ref[...] = acc * lax.rsqrt(ms + RMS_EPS) * g_ref[...]


def _sb_attention(q, k, v, g2):
    bsz, lp, _ = q.shape
    blk = SB_BLOCK
    npair = SB_WIDTH // LANES
    return pl.pallas_call(
        _sb_kernel,
        grid=(bsz, npair, lp // blk),
        in_specs=[pl.BlockSpec((None, blk, LANES), lambda b, p, i: (b, i, p)),
                  pl.BlockSpec((None, lp, LANES), lambda b, p, i: (b, 0, p)),
                  pl.BlockSpec((None, lp, LANES), lambda b, p, i: (b, 0, p)),
                  pl.BlockSpec((1, LANES), lambda b, p, i: (0, 0))],
        out_specs=pl.BlockSpec((None, blk, LANES), lambda b, p, i: (b, i, p)),
        out_shape=jax.ShapeDtypeStruct((bsz, lp, SB_WIDTH), F32),
        compiler_params=_cparams(("arbitrary", "arbitrary", "arbitrary")),
        name="sb_attn",
    )(q, k, v, g2)


def _gdn_kernel(alog_ref, dtb_ref, x_ref, z_ref, gt_ref, cw_ref, gn_ref, o_ref, s_ref, halo_ref, *, tm):
    c_len = GDN_CHUNK
    hd = GDN_HEAD_DIM
    j = pl.program_id(1)

    @pl.when(j == 0)
    def _():
        s_ref[...] = jnp.zeros_like(s_ref)
        halo_ref[...] = jnp.zeros_like(halo_ref)

    li = lax.broadcasted_iota(jnp.int32, (c_len, c_len), 0)
    lj = lax.broadcasted_iota(jnp.int32, (c_len, c_len), 1)
    lower_incl = li >= lj
    lower_strict = li > lj
    m_cum = jnp.concatenate([lower_incl.astype(BF16), jnp.ones((c_len, c_len), BF16)], axis=0)
    eye = (li == lj).astype(F32)
    lane = lax.broadcasted_iota(jnp.int32, (1, LANES), 1)

    def chunk(c, _):
        r0 = pl.multiple_of(c * c_len, c_len)
        prev = pl.multiple_of(jnp.maximum(r0 - 8, 0), 8)
        halo = jnp.where(c == 0, halo_ref[...], x_ref[pl.ds(prev, 8), :])
        xe = jnp.concatenate([halo, x_ref[pl.ds(r0, c_len), :]], axis=0)
        conv = xe[8:] * cw_ref[CONV_WIDTH - 1:CONV_WIDTH, :]
        for tap in range(CONV_WIDTH - 1):
            shift = CONV_WIDTH - 1 - tap
            conv = conv + pltpu.roll(xe, shift, axis=0)[8:] * cw_ref[tap:tap + 1, :]
        y = _silu(conv)
        gt = gt_ref[pl.ds(r0, c_len), :]
        rows = j * tm + r0 + lax.broadcasted_iota(jnp.int32, (c_len, 1), 0)
        valid = rows >= FRONT

        for h in range(GDN_HEADS):
            qr = y[:, h * hd:(h + 1) * hd]
            kr = y[:, GDN_WIDTH + h * hd:GDN_WIDTH + (h + 1) * hd]
            v = y[:, 2 * GDN_WIDTH + h * hd:2 * GDN_WIDTH + (h + 1) * hd]
            q = qr * lax.rsqrt(jnp.sum(qr * qr, axis=-1, keepdims=True) + RMS_EPS) * (hd ** -0.5)
            k = kr * lax.rsqrt(jnp.sum(kr * kr, axis=-1, keepdims=True) + RMS_EPS)
            b_in = jnp.sum(jnp.where(lane == h, gt, 0.0), axis=-1, keepdims=True)
            a_in = jnp.sum(jnp.where(lane == GDN_HEADS + h, gt, 0.0), axis=-1, keepdims=True)
            beta = jnp.where(valid, 1.0 / (1.0 + jnp.exp(-b_in)), 0.0)
            decay_rate = jnp.exp(jnp.full((1, 1), alog_ref[h], F32))
            g = jnp.where(valid, -decay_rate * _softplus(a_in + dtb_ref[h]), 0.0)

            gcat = _dot_sel(m_cum, jnp.broadcast_to(g, (c_len, LANES)))
            gc = gcat[:c_len]
            gtot = gcat[c_len:]
            g_between = jnp.where(lower_strict, jnp.broadcast_to(g, (c_len, c_len)), 0.0)
            diff = _dot_sel(lower_incl.astype(BF16), g_between)
            decay = jnp.where(lower_incl, jnp.exp(diff), 0.0)
            egc = jnp.exp(gc)
            e_end = jnp.exp(gtot - gc)
            cd = jnp.exp(gtot[0:1, :])

            kb = k * beta
            vb = v * beta
            k16 = k.astype(BF16)
            a_mat = jnp.where(lower_strict, _dot_nt(kb.astype(BF16), k16) * decay, 0.0)
            pw = a_mat
            t_inv = eye - a_mat
            for _ in range(5):
                p16 = pw.astype(BF16)
                pw = _dot(p16, p16)
                t_inv = t_inv + _dot(t_inv.astype(BF16), pw.astype(BF16))
            rhs = jnp.concatenate([kb * egc, vb], axis=1).astype(BF16)
            sol = _dot(t_inv.astype(BF16), rhs)
            w = sol[:, :hd]
            u = sol[:, hd:]
            qk = _dot_nt(q.astype(BF16), k16) * decay
            qd = q * egc
            ke = k * e_end

            s = s_ref[h]
            s16 = s.astype(BF16)
            vn = u - _dot(w.astype(BF16), s16)
            vn16 = vn.astype(BF16)
            o = _dot(qd.astype(BF16), s16) + _dot(qk.astype(BF16), vn16)
            s_ref[h] = s * cd + _dot_tn(ke.astype(BF16), vn16)

            zh = z_ref[pl.ds(r0, c_len), h * hd:(h + 1) * hd]
            on = o * lax.rsqrt(jnp.mean(o * o, axis=-1, keepdims=True) + RMS_EPS) * gn_ref[...]
            o_ref[pl.ds(r0, c_len), h * hd:(h + 1) * hd] = on * _silu(zh)
        return 0

    lax.fori_loop(0, tm // c_len, chunk, 0)
    halo_ref[...] = x_ref[tm - 8:tm, :]


def _gdn(a_log, dt_bias, gqkv, gz, gates, conv_w, gn, *, tm):
    bsz, lp, _ = gqkv.shape
    row = lambda w: pl.BlockSpec((None, tm, w), lambda i, j: (i, j, 0))
    smem = pl.BlockSpec(memory_space=pltpu.SMEM)
    return pl.pallas_call(
        functools.partial(_gdn_kernel, tm=tm),
        grid=(bsz, lp // tm),
        in_specs=[smem, smem, row(3 * GDN_WIDTH), row(GDN_WIDTH), row(LANES),
                  pl.BlockSpec((CONV_WIDTH, 3 * GDN_WIDTH), lambda i, j: (0, 0)),
                  pl.BlockSpec((1, GDN_HEAD_DIM), lambda i, j: (0, 0))],
        out_specs=row(GDN_WIDTH),
        out_shape=jax.ShapeDtypeStruct((bsz, lp, GDN_WIDTH), F32),
        scratch_shapes=[pltpu.VMEM((GDN_HEADS, GDN_HEAD_DIM, GDN_HEAD_DIM), F32),
                        pltpu.VMEM((8, 3 * GDN_WIDTH), F32)],
        compiler_params=_cparams(("arbitrary", "arbitrary")),
        name="gdn",
    )(a_log, dt_bias, gqkv, gz, gates, conv_w, gn)


def _route_kernel(osb_ref, ogdn_ref, h_ref, wo_ref, g_ref, b_ref, wr_ref, br_ref,
                  h1_ref, route_ref, cnt_ref, wob_ref, carry_ref, *, tm, l, alpha):
    first = (pl.program_id(0) == 0) & (pl.program_id(1) == 0)

    @pl.when(first)
    def _():
        wob_ref[...] = wo_ref[...].astype(BF16)
        carry_ref[...] = jnp.zeros_like(carry_ref)

    mix = (_dot(osb_ref[...].astype(BF16), wob_ref[0:SB_WIDTH, :])
           + _dot(ogdn_ref[...].astype(BF16), wob_ref[SB_WIDTH:, :]))
    y = _layer_norm(alpha * h_ref[...] + mix, g_ref[...], b_ref[...])
    h1 = jnp.where(_row_valid(pl.program_id(1), tm, l), y, 0.0)
    h1_ref[...] = h1

    lg = jnp.dot(h1, wr_ref[...], preferred_element_type=F32, precision=lax.Precision.HIGHEST) + br_ref[...]
    lane = lax.broadcasted_iota(jnp.int32, (1, LANES), 1).astype(F32)
    big = jnp.float32(1e9)
    gmask = lane < N_GROUPS
    gl = jnp.where(gmask, lg, NEG_BIG)
    gmax = jnp.max(gl, axis=-1, keepdims=True)
    gidx = jnp.min(jnp.where(gl == gmax, lane, big), axis=-1, keepdims=True)
    g_val = 1.0 / jnp.sum(jnp.where(gmask, jnp.exp(gl - gmax), 0.0), axis=-1, keepdims=True)
    lo = N_GROUPS + EXPERTS_PER_GROUP * gidx
    emask = (lane >= lo) & (lane < lo + EXPERTS_PER_GROUP)
    el = jnp.where(emask, lg, NEG_BIG)
    v1 = jnp.max(el, axis=-1, keepdims=True)
    i1 = jnp.min(jnp.where(el == v1, lane, big), axis=-1, keepdims=True)
    el2 = jnp.where(lane == i1, NEG_BIG, el)
    v2 = jnp.max(el2, axis=-1, keepdims=True)
    i2 = jnp.min(jnp.where(el2 == v2, lane, big), axis=-1, keepdims=True)
    e21 = jnp.exp(v2 - v1)
    w1 = 1.0 / (1.0 + e21)
    w2 = e21 * w1
    e1 = i1 - N_GROUPS
    e2 = i2 - N_GROUPS

    onehot = ((lane == e1) | (lane == e2)).astype(BF16)
    ri = lax.broadcasted_iota(jnp.int32, (tm, tm), 0)
    ci = lax.broadcasted_iota(jnp.int32, (tm, tm), 1)
    before = _dot((ri > ci).astype(BF16), onehot) + carry_ref[...]
    r1 = jnp.sum(jnp.where(lane == e1, before, 0.0), axis=-1, keepdims=True)
    r2 = jnp.sum(jnp.where(lane == e2, before, 0.0), axis=-1, keepdims=True)
    total = carry_ref[...] + jnp.sum(onehot.astype(F32), axis=0, keepdims=True)
    carry_ref[...] = total
    cnt_ref[...] = total

    out = jnp.zeros((tm, LANES), F32)
    for idx, val in enumerate((e1, e2, r1, r2, g_val * w1, g_val * w2)):
        out = jnp.where(lane == idx, val, out)
    route_ref[...] = out


def _route(o_sb, o_gdn, h, w_out, g, b, w_router, b_router, *, tm, l, alpha):
    bsz, lp, d = h.shape
    row = lambda w: pl.BlockSpec((None, tm, w), lambda i, j: (i, j, 0))
    full = lambda s: pl.BlockSpec(s, lambda i, j: (0, 0))
    return pl.pallas_call(
        functools.partial(_route_kernel, tm=tm, l=l, alpha=alpha),
        grid=(bsz, lp // tm),
        in_specs=[row(SB_WIDTH), row(GDN_WIDTH), row(d), full(w_out.shape), full((1, d)), full((1, d)),
                  full((d, LANES)), full((1, LANES))],
        out_specs=[row(d), row(LANES), full((1, LANES))],
        out_shape=[jax.ShapeDtypeStruct((bsz, lp, d), F32),
                   jax.ShapeDtypeStruct((bsz, lp, LANES), F32),
                   jax.ShapeDtypeStruct((1, LANES), F32)],
        scratch_shapes=[pltpu.VMEM(w_out.shape, BF16), pltpu.VMEM((1, LANES), F32)],
        compiler_params=_cparams(("arbitrary", "arbitrary")),
        name="outproj_ln1_route",
    )(o_sb, o_gdn, h, w_out, g, b, w_router, b_router)


def _row_copy(src, s, dst, d, sem):
    return pltpu.make_async_copy(src.at[pl.ds(s, 1)], dst.at[pl.ds(d, 1)], sem)


def _dispatch_kernel(ids_ref, base_ref, h_ref, xs_in_ref, xs_ref, sem, *, tm):
    del xs_in_ref

    def issue(t, _):
        _row_copy(h_ref, t, xs_ref, base_ref[ids_ref[0, t]] + ids_ref[2, t], sem).start()
        _row_copy(h_ref, t, xs_ref, base_ref[ids_ref[1, t]] + ids_ref[3, t], sem).start()
        return 0

    lax.fori_loop(0, tm, issue, 0)
    for _ in range(2):
        pltpu.make_async_copy(h_ref, xs_ref.at[pl.ds(0, tm)], sem).wait()


def _dispatch(ids, base, h1_flat, xs_zero, *, tm):
    tp, d = h1_flat.shape
    return pl.pallas_call(
        functools.partial(_dispatch_kernel, tm=tm),
        grid=(tp // tm,),
        in_specs=[pl.BlockSpec((None, 4, tm), lambda i: (i, 0, 0), memory_space=pltpu.SMEM),
                  pl.BlockSpec(memory_space=pltpu.SMEM),
                  pl.BlockSpec((tm, d), lambda i: (i, 0)),
                  pl.BlockSpec(memory_space=pl.ANY)],
        out_specs=pl.BlockSpec(memory_space=pl.ANY),
        out_shape=jax.ShapeDtypeStruct(xs_zero.shape, F32),
        scratch_shapes=[pltpu.SemaphoreType.DMA(())],
        input_output_aliases={3: 0},
        compiler_params=_cparams(("arbitrary",)),
        name="moe_dispatch",
    )(ids, base, h1_flat, xs_zero)


def _expert_kernel(te_ref, nused_ref, xs_ref, w1_ref, w3_ref, w2_ref, ys_ref):
    i = pl.program_id(0)

    @pl.when(i < nused_ref[0])
    def _():
        x = xs_ref[...].astype(BF16)
        a = _dot(x, w1_ref[...].astype(BF16))
        b = _dot(x, w3_ref[...].astype(BF16))
        hmid = (_silu(a) * b).astype(BF16)
        ys_ref[...] = _dot(hmid, w2_ref[...].astype(BF16))

    @pl.when(i >= nused_ref[0])
    def _():
        ys_ref[...] = jnp.zeros_like(ys_ref)


def _experts(tile_expert, n_used, xs, w1, w3, w2):
    rows, d = xs.shape
    tme = EXPERT_TILE
    f = w1.shape[-1]
    grid_spec = pltpu.PrefetchScalarGridSpec(
        num_scalar_prefetch=2,
        grid=(rows // tme,),
        in_specs=[pl.BlockSpec((tme, d), lambda i, te, nu: (i, 0)),
                  pl.BlockSpec((None, d, f), lambda i, te, nu: (te[i], 0, 0)),
                  pl.BlockSpec((None, d, f), lambda i, te, nu: (te[i], 0, 0)),
                  pl.BlockSpec((None, f, d), lambda i, te, nu: (te[i], 0, 0))],
        out_specs=pl.BlockSpec((tme, d), lambda i, te, nu: (i, 0)),
    )
    return pl.pallas_call(
        _expert_kernel,
        grid_spec=grid_spec,
        out_shape=jax.ShapeDtypeStruct((rows, d), F32),
        compiler_params=_cparams(("arbitrary",)),
        name="moe_experts",
    )(tile_expert, n_used, xs, w1, w3, w2)


def _combine_kernel(ids_ref, base_ref, route_ref, h1_ref, g_ref, b_ref, ys_ref,
                    o_ref, y0_ref, y1_ref, sem, *, tm, l, alpha):
    def issue(t, _):
        _row_copy(ys_ref, base_ref[ids_ref[0, t]] + ids_ref[2, t], y0_ref, t, sem).start()
        _row_copy(ys_ref, base_ref[ids_ref[1, t]] + ids_ref[3, t], y1_ref, t, sem).start()
        return 0

    lax.fori_loop(0, tm, issue, 0)
    pltpu.make_async_copy(ys_ref.at[pl.ds(0, tm)], y0_ref, sem).wait()
    pltpu.make_async_copy(ys_ref.at[pl.ds(0, tm)], y1_ref, sem).wait()

    lane = lax.broadcasted_iota(jnp.int32, (1, LANES), 1)
    route = route_ref[...]
    gate1 = jnp.sum(jnp.where(lane == 4, route, 0.0), axis=-1, keepdims=True)
    gate2 = jnp.sum(jnp.where(lane == 5, route, 0.0), axis=-1, keepdims=True)
    ffn = gate1 * y0_ref[...] + gate2 * y1_ref[...]
    y = _layer_norm(alpha * h1_ref[...] + ffn, g_ref[...], b_ref[...])
    o_ref[...] = jnp.where(_row_valid(pl.program_id(1), tm, l), y, 0.0)


def _combine(ids, base, route, h1, g, b, ys, *, tm, l, alpha):
    bsz, lp, d = h1.shape
    nj = lp // tm
    row = lambda w: pl.BlockSpec((None, tm, w), lambda i, j: (i, j, 0))
    full = lambda s: pl.BlockSpec(s, lambda i, j: (0, 0))
    return pl.pallas_call(
        functools.partial(_combine_kernel, tm=tm, l=l, alpha=alpha),
        grid=(bsz, nj),
        in_specs=[pl.BlockSpec((None, 4, tm), lambda i, j: (i * nj + j, 0, 0), memory_space=pltpu.SMEM),
                  pl.BlockSpec(memory_space=pltpu.SMEM),
                  row(LANES), row(d), full((1, d)), full((1, d)), pl.BlockSpec(memory_space=pl.ANY)],
        out_specs=row(d),
        out_shape=jax.ShapeDtypeStruct((bsz, lp, d), F32),
        scratch_shapes=[pltpu.VMEM((tm, d), F32), pltpu.VMEM((tm, d), F32), pltpu.SemaphoreType.DMA(())],
        compiler_params=_cparams(("arbitrary", "arbitrary")),
        name="moe_combine_ln2",
    )(ids, base, route, h1, g, b, ys)


def kernel(x, meta_tokens, ln_in_g, ln_in_b, w_in, conv_w, a_log, dt_bias, sb_norm_g, gdn_norm_g, w_out,
           ln1_g, ln1_b, w_group, b_group, w_expert, b_expert, w1, w3, w2, ln2_g, ln2_b):
    bsz, seq, d = x.shape
    depth = w_in.shape[0]
    l = seq + N_META
    lp = -(-(FRONT + l) // LANES) * LANES
    tm = 384 if lp % 384 == 0 else LANES
    tp = bsz * lp
    alpha = float((2 * depth) ** 0.25)
    n_tiles = 2 * tp // EXPERT_TILE + N_EXPERTS
    row2 = lambda a: a.reshape(1, -1)

    meta = jnp.broadcast_to(meta_tokens.astype(x.dtype)[None], (bsz, N_META, d))
    xp = jnp.concatenate([jnp.zeros((bsz, FRONT, d), x.dtype), meta, x,
                          jnp.zeros((bsz, lp - FRONT - l, d), x.dtype)], axis=1)
    h = _ln_in(xp, row2(ln_in_g), row2(ln_in_b), tm=tm, l=l)

    n_main = 3 * SB_WIDTH + 4 * GDN_WIDTH
    for i in range(depth):
        w_main = w_in[i, :, :n_main]
        w_small = jnp.pad(w_in[i, :, n_main:], ((0, 0), (0, LANES - 2 * GDN_HEADS)))
        q, k, v, gqkv, gz, gates = _inproj(h, w_main, w_small, tm=tm)
        o_sb = _sb_attention(q, k, v, row2(jnp.tile(sb_norm_g[i], 2)))
        o_gdn = _gdn(a_log[i], dt_bias[i], gqkv, gz, gates, conv_w[i], row2(gdn_norm_g[i]), tm=tm)

        w_router = jnp.pad(jnp.concatenate([w_group[i], w_expert[i]], axis=1),
                           ((0, 0), (0, LANES - N_GROUPS - N_EXPERTS)))
        b_router = jnp.pad(jnp.concatenate([b_group[i], b_expert[i]]), (0, LANES - N_GROUPS - N_EXPERTS))
        h1, route, counts = _route(o_sb, o_gdn, h, w_out[i], row2(ln1_g[i]), row2(ln1_b[i]),
                                   w_router, row2(b_router), tm=tm, l=l, alpha=alpha)

        ids = route[..., :4].astype(jnp.int32).reshape(tp // tm, tm, 4).transpose(0, 2, 1)
        cnt = counts[0, :N_EXPERTS].astype(jnp.int32)
        tiles = (cnt + EXPERT_TILE - 1) // EXPERT_TILE
        tile_end = jnp.cumsum(tiles)
        base = (tile_end - tiles) * EXPERT_TILE
        tile_expert = jnp.minimum(
            jnp.sum(jnp.arange(n_tiles, dtype=jnp.int32)[:, None] >= tile_end[None, :], axis=1),
            N_EXPERTS - 1).astype(jnp.int32)
        n_used = tile_end[-1:].astype(jnp.int32)

        xs = _dispatch(ids, base, h1.reshape(tp, d), jnp.zeros((n_tiles * EXPERT_TILE, d), F32), tm=tm)
        ys = _experts(tile_expert, n_used, xs, w1[i], w3[i], w2[i])
        h = _combine(ids, base, route, h1, row2(ln2_g[i]), row2(ln2_b[i]), ys, tm=tm, l=l, alpha=alpha)

    return h[:, FRONT + N_META:FRONT + l]
```

```python
import functools

import jax
import jax.numpy as jnp
from jax import lax
from jax.experimental import pallas as pl
from jax.experimental.pallas import tpu as pltpu

N_META = 16
SB_HEADS = 8
SB_HEAD_DIM = 64
SB_WIDTH = SB_HEADS * SB_HEAD_DIM
GDN_HEADS = 4
GDN_HEAD_DIM = 128
GDN_WIDTH = GDN_HEADS * GDN_HEAD_DIM
GDN_CHUNK = 64
CONV_WIDTH = 4
N_GROUPS = 4
EXPERTS_PER_GROUP = 8
N_EXPERTS = N_GROUPS * EXPERTS_PER_GROUP
D_EXPERT = 256
LN_EPS = 1e-5
RMS_EPS = 1e-6

LANES = 128
SB_BLOCK = 128
FRONT = (-N_META) % GDN_CHUNK
EXPERT_TILE = 256
VMEM_LIMIT = 56 * 1024 * 1024
EXP_UNDERFLOW = -88.0
NEG_BIG = -1e30

F32 = jnp.float32
BF16 = jnp.bfloat16


def _cparams(sem):
    return pltpu.CompilerParams(dimension_semantics=sem, vmem_limit_bytes=VMEM_LIMIT)


def _dot(a, b):
    return jnp.dot(a, b, preferred_element_type=F32)


def _dot_nt(a, b):
    return lax.dot_general(a, b, (((1,), (1,)), ((), ())), preferred_element_type=F32)


def _dot_tn(a, b):
    return lax.dot_general(a, b, (((0,), (0,)), ((), ())), preferred_element_type=F32)


def _split3(x):
    hi = x.astype(BF16)
    r1 = x - hi.astype(F32)
    mid = r1.astype(BF16)
    lo = (r1 - mid.astype(F32)).astype(BF16)
    return hi, mid, lo


def _dot_sel(m01, x):
    hi, mid, lo = _split3(x)
    return _dot(m01, hi) + _dot(m01, mid) + _dot(m01, lo)


def _silu(x):
    return x / (1.0 + jnp.exp(-x))


def _softplus(x):
    return jnp.maximum(x, 0.0) + jnp.log1p(jnp.exp(-jnp.abs(x)))


def _layer_norm(x, g, b):
    mu = jnp.mean(x, axis=-1, keepdims=True)
    xc = x - mu
    var = jnp.mean(xc * xc, axis=-1, keepdims=True)
    return xc * lax.rsqrt(var + LN_EPS) * g + b


def _row_valid(j, tm, l):
    r = j * tm + lax.broadcasted_iota(jnp.int32, (tm, 1), 0)
    return (r >= FRONT) & (r < FRONT + l)


def _ln_in_kernel(x_ref, g_ref, b_ref, o_ref, *, tm, l):
    y = _layer_norm(x_ref[...], g_ref[...], b_ref[...])
    o_ref[...] = jnp.where(_row_valid(pl.program_id(1), tm, l), y, 0.0)


def _ln_in(xp, g, b, *, tm, l):
    bsz, lp, d = xp.shape
    return pl.pallas_call(
        functools.partial(_ln_in_kernel, tm=tm, l=l),
        grid=(bsz, lp // tm),
        in_specs=[pl.BlockSpec((None, tm, d), lambda i, j: (i, j, 0)),
                  pl.BlockSpec((1, d), lambda i, j: (0, 0)),
                  pl.BlockSpec((1, d), lambda i, j: (0, 0))],
        out_specs=pl.BlockSpec((None, tm, d), lambda i, j: (i, j, 0)),
        out_shape=jax.ShapeDtypeStruct((bsz, lp, d), F32),
        compiler_params=_cparams(("arbitrary", "arbitrary")),
        name="ln_in",
    )(xp, g, b)


def _inproj_kernel(x_ref, w_ref, ws_ref, q_ref, k_ref, v_ref, g_ref, z_ref, gt_ref, wb_ref, wsb_ref):
    first = (pl.program_id(0) == 0) & (pl.program_id(1) == 0)

    @pl.when(first)
    def _():
        wb_ref[...] = w_ref[...].astype(BF16)
        wsb_ref[...] = ws_ref[...].astype(BF16)

    x = x_ref[...].astype(BF16)
    s1, s2, s3 = SB_WIDTH, 2 * SB_WIDTH, 3 * SB_WIDTH
    s4 = s3 + 3 * GDN_WIDTH
    s5 = s4 + GDN_WIDTH
    q_ref[...] = _dot(x, wb_ref[:, 0:s1]).astype(BF16)
    k_ref[...] = _dot(x, wb_ref[:, s1:s2]).astype(BF16)
    v_ref[...] = _dot(x, wb_ref[:, s2:s3]).astype(BF16)
    g_ref[...] = _dot(x, wb_ref[:, s3:s4])
    z_ref[...] = _dot(x, wb_ref[:, s4:s5])
    gt_ref[...] = _dot(x, wsb_ref[...])


def _inproj(h, w_in, w_small, layer, *, tm):
    bsz, lp, d = h.shape
    n_main = 3 * SB_WIDTH + 4 * GDN_WIDTH
    row = lambda w: pl.BlockSpec((None, tm, w), lambda i, j: (i, j, 0))
    shp = lambda w, dt: jax.ShapeDtypeStruct((bsz, lp, w), dt)
    return pl.pallas_call(
        _inproj_kernel,
        grid=(bsz, lp // tm),
        in_specs=[row(d),
                  pl.BlockSpec((None, d, n_main), lambda i, j: (layer, 0, 0)),
                  pl.BlockSpec((d, LANES), lambda i, j: (0, 0))],
        out_specs=[row(SB_WIDTH), row(SB_WIDTH), row(SB_WIDTH), row(3 * GDN_WIDTH), row(GDN_WIDTH), row(LANES)],
        out_shape=[shp(SB_WIDTH, BF16), shp(SB_WIDTH, BF16), shp(SB_WIDTH, BF16),
                   shp(3 * GDN_WIDTH, F32), shp(GDN_WIDTH, F32), shp(LANES, F32)],
        scratch_shapes=[pltpu.VMEM((d, n_main), BF16), pltpu.VMEM((d, LANES), BF16)],
        compiler_params=_cparams(("arbitrary", "arbitrary")),
        name="inproj",
    )(h, w_in, w_small)


def _sb_scores(qs, kw, vw, vis, ucat, carry):
    n = len(qs)
    nk = kw[0].shape[0]
    z = [_dot_nt(qs[i], kw[i]) for i in range(n)]
    lnb = [-(jnp.maximum(z[i], 0.0) + jnp.log(1.0 + jnp.exp(-jnp.abs(z[i])))) for i in range(n)]
    if vis is not None:
        lnb = [jnp.where(vis[i], lnb[i], 0.0) for i in range(n)]
    hi = [x.astype(BF16) for x in lnb]
    lo = [(lnb[i] - hi[i].astype(F32)).astype(BF16) for i in range(n)]
    t = [_dot(jnp.concatenate([hi[i], lo[i]], axis=0), ucat) for i in range(n)]
    t = [x[:2 * SB_BLOCK] + x[2 * SB_BLOCK:] for x in t]
    out = []
    p16 = []
    for i in range(n):
        between = t[i][:, :nk] - lnb[i]
        if carry is not None:
            between = between + carry[i]
        logw = z[i] + lnb[i] + between
        p = jnp.exp(logw) if vis is None else jnp.where(vis[i], jnp.exp(logw), 0.0)
        p16.append(p.astype(BF16))
    pv = [_dot(p16[i], vw[i]) for i in range(n)]
    return [(pv[i], t[i][:, nk:]) for i in range(n)]


def _sb_kernel(q_ref, k_ref, v_ref, g_ref, o_ref, *, qb):
    blk = SB_BLOCK
    win = 2 * blk
    step = pl.program_id(2)
    head0 = lax.broadcasted_iota(jnp.int32, (1, LANES), 1) < SB_HEAD_DIM
    zero = jnp.zeros((), BF16)

    def selector(n):
        r = lax.broadcasted_iota(jnp.int32, (n, n), 0)
        c = lax.broadcasted_iota(jnp.int32, (n, n), 1)
        return jnp.concatenate([(r >= c).astype(BF16), jnp.ones((n, LANES), BF16)], axis=1)

    ucat_win = selector(win)
    ucat_blk = selector(blk)
    row = lax.broadcasted_iota(jnp.int32, (blk, win), 0)
    col = lax.broadcasted_iota(jnp.int32, (blk, win), 1)

    def merge(pv):
        return jnp.where(head0, pv[:blk], pv[blk:])

    ibs, qss, kws, vws, viss = [], [], [], [], []
    for b in range(qb):
        ib = step * qb + b
        s0 = pl.multiple_of(jnp.maximum(ib - 1, 0) * blk, blk)
        q2 = q_ref[b * blk:(b + 1) * blk, :] * jnp.asarray(SB_HEAD_DIM ** -0.5, BF16)
        vis1 = (s0 + col) < (ib * blk + row)
        ibs.append(ib)
        qss.append(jnp.concatenate([jnp.where(head0, q2, zero), jnp.where(head0, zero, q2)], axis=0))
        kws.append(k_ref[pl.ds(s0, win), :])
        vws.append(v_ref[pl.ds(s0, win), :])
        viss.append(jnp.concatenate([vis1, vis1], axis=0))
    first = _sb_scores(qss, kws, vws, viss, ucat_win, None)

    for b, (pv0, tot) in enumerate(first):
        ib, qs, acc = ibs[b], qss[b], merge(pv0)

        def body(carry, qs=qs):
            j, c, a, _ = carry
            start = pl.multiple_of(j * blk, blk)
            (pv, t), = _sb_scores([qs], [k_ref[pl.ds(start, blk), :]], [v_ref[pl.ds(start, blk), :]],
                                  None, ucat_blk, [c])
            c = c + t
            return j - 1, c, a + merge(pv), jnp.max(c)

        def cond(carry):
            j, _, _, cmax = carry
            return (j >= 0) & (cmax >= EXP_UNDERFLOW)

        _, _, acc, _ = lax.while_loop(cond, body, (jnp.maximum(ib - 1, 0) - 1, tot, acc, jnp.max(tot)))

        sq = acc * acc
        ms0 = jnp.sum(jnp.where(head0, sq, 0.0), axis=-1, keepdims=True)
        ms1 = jnp.sum(jnp.where(head0, 0.0, sq), axis=-1, keepdims=True)
        ms = jnp.where(head0, ms0, ms1) * (1.0 / SB_HEAD_DIM)
        o_ref[b * blk:(b + 1) * blk, :] = acc * lax.rsqrt(ms + RMS_EPS) * g_ref[...]


def _sb_attention(q, k, v, g2):
    bsz, lp, _ = q.shape
    blk = SB_BLOCK
    nblk = lp // blk
    qb = 3 if nblk % 3 == 0 else (2 if nblk % 2 == 0 else 1)
    npair = SB_WIDTH // LANES
    return pl.pallas_call(
        functools.partial(_sb_kernel, qb=qb),
        grid=(bsz, npair, nblk // qb),
        in_specs=[pl.BlockSpec((None, qb * blk, LANES), lambda b, p, i: (b, i, p)),
                  pl.BlockSpec((None, lp, LANES), lambda b, p, i: (b, 0, p)),
                  pl.BlockSpec((None, lp, LANES), lambda b, p, i: (b, 0, p)),
                  pl.BlockSpec((1, LANES), lambda b, p, i: (0, 0))],
        out_specs=pl.BlockSpec((None, qb * blk, LANES), lambda b, p, i: (b, i, p)),
        out_shape=jax.ShapeDtypeStruct((bsz, lp, SB_WIDTH), F32),
        compiler_params=_cparams(("arbitrary", "arbitrary", "arbitrary")),
        name="sb_attn",
    )(q, k, v, g2)


def _gdn_kernel(alog_ref, dtb_ref, x_ref, z_ref, gt_ref, cw_ref, gn_ref, o_ref,
                s_ref, halo_ref, qe_ref, oi_ref, xm_ref, nn_ref, cd_ref, *, tm):
    c_len = GDN_CHUNK
    hd = GDN_HEAD_DIM
    nch = tm // c_len
    j = pl.program_id(1)

    @pl.when(j == 0)
    def _():
        s_ref[...] = jnp.zeros_like(s_ref)
        halo_ref[...] = jnp.zeros_like(halo_ref)

    li = lax.broadcasted_iota(jnp.int32, (c_len, c_len), 0)
    lj = lax.broadcasted_iota(jnp.int32, (c_len, c_len), 1)
    lower_incl = li >= lj
    lower_strict = li > lj
    m_cum = jnp.concatenate([lower_incl.astype(BF16), jnp.ones((c_len, c_len), BF16)], axis=0)
    eye = (li == lj).astype(F32)
    lane = lax.broadcasted_iota(jnp.int32, (1, LANES), 1)

    gate_lane = (lane >= GDN_HEADS) & (lane < 2 * GDN_HEADS)
    alog_lane = jnp.zeros((1, LANES), F32)
    dtb_lane = jnp.zeros((1, LANES), F32)
    for h in range(GDN_HEADS):
        alog_lane = jnp.where(lane == GDN_HEADS + h, alog_ref[h], alog_lane)
        dtb_lane = jnp.where(lane == GDN_HEADS + h, dtb_ref[h], dtb_lane)
    rate_lane = jnp.exp(alog_lane)

    def column(x, idx):
        return jnp.sum(jnp.where(lane == idx, x, 0.0), axis=-1, keepdims=True)

    def prepare(chunks):
        per_chunk = []
        for c in chunks:
            if isinstance(c, int):
                r0 = c * c_len
                halo = halo_ref[...] if c == 0 else x_ref[r0 - 8:r0, :]
            else:
                r0 = pl.multiple_of(c * c_len, c_len)
                halo = x_ref[pl.ds(pl.multiple_of(r0 - 8, 8), 8), :]
            xe = jnp.concatenate([halo, x_ref[pl.ds(r0, c_len), :]], axis=0)
            conv = xe[8:] * cw_ref[CONV_WIDTH - 1:CONV_WIDTH, :]
            for tap in range(CONV_WIDTH - 1):
                shift = CONV_WIDTH - 1 - tap
                conv = conv + pltpu.roll(xe, shift, axis=0)[8:] * cw_ref[tap:tap + 1, :]
            y = _silu(conv)
            gt = gt_ref[pl.ds(r0, c_len), :]
            rows = j * tm + r0 + lax.broadcasted_iota(jnp.int32, (c_len, 1), 0)
            valid = rows >= FRONT
            beta_all = jnp.where(valid, 1.0 / (1.0 + jnp.exp(-gt)), 0.0)
            g_all = jnp.where(valid & gate_lane, -rate_lane * _softplus(gt + dtb_lane), 0.0)
            per_chunk.append((c, y, beta_all, g_all))

        gcats = [_dot_sel(m_cum, g_all) for (_, _, _, g_all) in per_chunk]
        gc_rows = [gcat[:c_len].T for gcat in gcats]

        probs = []
        for (c, y, beta_all, _), gcat, gc_row in zip(per_chunk, gcats, gc_rows):
            for h in range(GDN_HEADS):
                qr = y[:, h * hd:(h + 1) * hd]
                kr = y[:, GDN_WIDTH + h * hd:GDN_WIDTH + (h + 1) * hd]
                v = y[:, 2 * GDN_WIDTH + h * hd:2 * GDN_WIDTH + (h + 1) * hd]
                q = qr * lax.rsqrt(jnp.sum(qr * qr, axis=-1, keepdims=True) + RMS_EPS) * (hd ** -0.5)
                k = kr * lax.rsqrt(jnp.sum(kr * kr, axis=-1, keepdims=True) + RMS_EPS)
                beta = column(beta_all, h)
                gc = column(gcat[:c_len], GDN_HEADS + h)
                gtot = column(gcat[c_len:], GDN_HEADS + h)
                diff = gc - gc_row[GDN_HEADS + h:GDN_HEADS + h + 1, :]
                decay = jnp.where(lower_incl, jnp.exp(jnp.where(lower_incl, diff, 0.0)), 0.0)
                kb = k * beta
                probs.append(dict(
                    slot=c * GDN_HEADS + h, q=q, k=k, k16=k.astype(BF16), kb=kb, decay=decay,
                    rhs=jnp.concatenate([kb * jnp.exp(gc), v * beta], axis=1).astype(BF16),
                    qd=q * jnp.exp(gc), ke16=(k * jnp.exp(gtot - gc)).astype(BF16),
                    cd=jnp.exp(gtot[0:1, :])))

        for p in probs:
            kq = _dot_nt(jnp.concatenate([p["kb"], p["q"]], axis=0).astype(BF16), p["k16"])
            p["a"] = jnp.where(lower_strict, kq[:c_len] * p["decay"], 0.0)
            p["qk16"] = (kq[c_len:] * p["decay"]).astype(BF16)
            p["pw"] = p["a"]
            p["t"] = eye - p["a"]
        for _ in range(5):
            for p in probs:
                p16 = p["pw"].astype(BF16)
                p["pw"] = _dot(p16, p16)
            for p in probs:
                p["t"] = p["t"] + _dot(p["t"].astype(BF16), p["pw"].astype(BF16))
        for p in probs:
            p["wu"] = _dot(p["t"].astype(BF16), p["rhs"]).astype(BF16)
        for p in probs:
            p["qkwu"] = _dot(p["qk16"], p["wu"])
        for p in probs:
            p["kewu"] = _dot_tn(p["ke16"], p["wu"])
        for p in probs:
            slot = p["slot"]
            qe_ref[slot] = (p["qd"] - p["qkwu"][:, :hd]).astype(BF16)
            oi_ref[slot] = p["qkwu"][:, hd:]
            xm_ref[slot] = (-p["kewu"][:, :hd]).astype(BF16)
            nn_ref[slot] = p["kewu"][:, hd:]
            cd_ref[slot] = jnp.broadcast_to(p["cd"], (8, LANES))

    def scan(c):
        r0 = c * c_len if isinstance(c, int) else pl.multiple_of(c * c_len, c_len)
        heads = range(GDN_HEADS)
        slots = [c * GDN_HEADS + h for h in heads]
        s = [s_ref[h] for h in heads]
        s16 = [x.astype(BF16) for x in s]
        so = [_dot(qe_ref[slots[h]], s16[h]) for h in heads]
        sx = [_dot(xm_ref[slots[h]], s16[h]) for h in heads]
        for h in heads:
            s_ref[h] = s[h] * cd_ref[slots[h]][0:1, :] + sx[h] + nn_ref[slots[h]]
        for h in heads:
            o = so[h] + oi_ref[slots[h]]
            zh = z_ref[pl.ds(r0, c_len), h * hd:(h + 1) * hd]
            on = o * lax.rsqrt(jnp.mean(o * o, axis=-1, keepdims=True) + RMS_EPS) * gn_ref[...]
            o_ref[pl.ds(r0, c_len), h * hd:(h + 1) * hd] = on * _silu(zh)

    prepare([0, 1])

    def pipelined(it, _):
        scan(2 * it - 2)
        scan(2 * it - 1)
        prepare([2 * it, 2 * it + 1])
        return 0

    lax.fori_loop(1, nch // 2, pipelined, 0)
    scan(nch - 2)
    scan(nch - 1)
    halo_ref[...] = x_ref[tm - 8:tm, :]


def _gdn(a_log, dt_bias, gqkv, gz, gates, conv_w, gn, *, tm):
    bsz, lp, _ = gqkv.shape
    nslot = (tm // GDN_CHUNK) * GDN_HEADS
    row = lambda w: pl.BlockSpec((None, tm, w), lambda i, j: (i, j, 0))
    smem = pl.BlockSpec(memory_space=pltpu.SMEM)
    return pl.pallas_call(
        functools.partial(_gdn_kernel, tm=tm),
        grid=(bsz, lp // tm),
        in_specs=[smem, smem, row(3 * GDN_WIDTH), row(GDN_WIDTH), row(LANES),
                  pl.BlockSpec((CONV_WIDTH, 3 * GDN_WIDTH), lambda i, j: (0, 0)),
                  pl.BlockSpec((1, GDN_HEAD_DIM), lambda i, j: (0, 0))],
        out_specs=row(GDN_WIDTH),
        out_shape=jax.ShapeDtypeStruct((bsz, lp, GDN_WIDTH), F32),
        scratch_shapes=[pltpu.VMEM((GDN_HEADS, GDN_HEAD_DIM, GDN_HEAD_DIM), F32),
                        pltpu.VMEM((8, 3 * GDN_WIDTH), F32),
                        pltpu.VMEM((nslot, GDN_CHUNK, GDN_HEAD_DIM), BF16),
                        pltpu.VMEM((nslot, GDN_CHUNK, GDN_HEAD_DIM), F32),
                        pltpu.VMEM((nslot, GDN_HEAD_DIM, GDN_HEAD_DIM), BF16),
                        pltpu.VMEM((nslot, GDN_HEAD_DIM, GDN_HEAD_DIM), F32),
                        pltpu.VMEM((nslot, 8, LANES), F32)],
        compiler_params=_cparams(("arbitrary", "arbitrary")),
        name="gdn",
    )(a_log, dt_bias, gqkv, gz, gates, conv_w, gn)


def _route_kernel(osb_ref, ogdn_ref, h_ref, wo_ref, g_ref, b_ref, wr_ref, br_ref,
                  h1_ref, route_ref, cnt_ref, wob_ref, carry_ref, *, tm, l, alpha):
    first = (pl.program_id(0) == 0) & (pl.program_id(1) == 0)

    @pl.when(first)
    def _():
        wob_ref[...] = wo_ref[...].astype(BF16)
        carry_ref[...] = jnp.zeros_like(carry_ref)

    mix = (_dot(osb_ref[...].astype(BF16), wob_ref[0:SB_WIDTH, :])
           + _dot(ogdn_ref[...].astype(BF16), wob_ref[SB_WIDTH:, :]))
    y = _layer_norm(alpha * h_ref[...] + mix, g_ref[...], b_ref[...])
    h1 = jnp.where(_row_valid(pl.program_id(1), tm, l), y, 0.0)
    h1_ref[...] = h1

    lg = jnp.dot(h1, wr_ref[...], preferred_element_type=F32, precision=lax.Precision.HIGHEST) + br_ref[...]
    lane = lax.broadcasted_iota(jnp.int32, (1, LANES), 1).astype(F32)
    big = jnp.float32(1e9)
    gmask = lane < N_GROUPS
    gl = jnp.where(gmask, lg, NEG_BIG)
    gmax = jnp.max(gl, axis=-1, keepdims=True)
    gidx = jnp.min(jnp.where(gl == gmax, lane, big), axis=-1, keepdims=True)
    g_val = 1.0 / jnp.sum(jnp.where(gmask, jnp.exp(gl - gmax), 0.0), axis=-1, keepdims=True)
    lo = N_GROUPS + EXPERTS_PER_GROUP * gidx
    emask = (lane >= lo) & (lane < lo + EXPERTS_PER_GROUP)
    el = jnp.where(emask, lg, NEG_BIG)
    v1 = jnp.max(el, axis=-1, keepdims=True)
    i1 = jnp.min(jnp.where(el == v1, lane, big), axis=-1, keepdims=True)
    el2 = jnp.where(lane == i1, NEG_BIG, el)
    v2 = jnp.max(el2, axis=-1, keepdims=True)
    i2 = jnp.min(jnp.where(el2 == v2, lane, big), axis=-1, keepdims=True)
    e21 = jnp.exp(v2 - v1)
    w1 = 1.0 / (1.0 + e21)
    w2 = e21 * w1
    e1 = i1 - N_GROUPS
    e2 = i2 - N_GROUPS

    onehot = ((lane == e1) | (lane == e2)).astype(BF16)
    ri = lax.broadcasted_iota(jnp.int32, (tm, tm), 0)
    ci = lax.broadcasted_iota(jnp.int32, (tm, tm), 1)
    before = _dot((ri > ci).astype(BF16), onehot) + carry_ref[...]
    r1 = jnp.sum(jnp.where(lane == e1, before, 0.0), axis=-1, keepdims=True)
    r2 = jnp.sum(jnp.where(lane == e2, before, 0.0), axis=-1, keepdims=True)
    total = carry_ref[...] + jnp.sum(onehot.astype(F32), axis=0, keepdims=True)
    carry_ref[...] = total
    cnt_ref[...] = total

    out = jnp.zeros((tm, LANES), F32)
    for idx, val in enumerate((e1, e2, r1, r2, g_val * w1, g_val * w2)):
        out = jnp.where(lane == idx, val, out)
    route_ref[...] = out


def _route(o_sb, o_gdn, h, w_out, g, b, w_router, b_router, *, tm, l, alpha):
    bsz, lp, d = h.shape
    row = lambda w: pl.BlockSpec((None, tm, w), lambda i, j: (i, j, 0))
    full = lambda s: pl.BlockSpec(s, lambda i, j: (0, 0))
    return pl.pallas_call(
        functools.partial(_route_kernel, tm=tm, l=l, alpha=alpha),
        grid=(bsz, lp // tm),
        in_specs=[row(SB_WIDTH), row(GDN_WIDTH), row(d), full(w_out.shape), full((1, d)), full((1, d)),
                  full((d, LANES)), full((1, LANES))],
        out_specs=[row(d), row(LANES), full((1, LANES))],
        out_shape=[jax.ShapeDtypeStruct((bsz, lp, d), F32),
                   jax.ShapeDtypeStruct((bsz, lp, LANES), F32),
                   jax.ShapeDtypeStruct((1, LANES), F32)],
        scratch_shapes=[pltpu.VMEM(w_out.shape, BF16), pltpu.VMEM((1, LANES), F32)],
        compiler_params=_cparams(("arbitrary", "arbitrary")),
        name="outproj_ln1_route",
    )(o_sb, o_gdn, h, w_out, g, b, w_router, b_router)


def _row_copy(src, s, dst, d, sem):
    return pltpu.make_async_copy(src.at[pl.ds(s, 1)], dst.at[pl.ds(d, 1)], sem)


def _dispatch_kernel(pos_ref, tile_end_ref, h_ref, xs_ref, zero_ref, sem, *, tm):

    @pl.when(pl.program_id(0) == 0)
    def _():
        zero_ref[...] = jnp.zeros_like(zero_ref)

        def fill(e, wait):
            first_tile = jnp.where(e == 0, 0, tile_end_ref[jnp.maximum(e - 1, 0)])

            @pl.when(tile_end_ref[e] > first_tile)
            def _():
                dst = pl.multiple_of((tile_end_ref[e] - 1) * EXPERT_TILE, EXPERT_TILE)
                cp = pltpu.make_async_copy(zero_ref, xs_ref.at[pl.ds(dst, EXPERT_TILE)], sem)
                if wait:
                    cp.wait()
                else:
                    cp.start()
            return 0

        def fill_unused(i, wait):
            dst = pl.multiple_of(i * EXPERT_TILE, EXPERT_TILE)
            cp = pltpu.make_async_copy(zero_ref, xs_ref.at[pl.ds(dst, EXPERT_TILE)], sem)
            cp.wait() if wait else cp.start()
            return 0

        n_used = tile_end_ref[N_EXPERTS - 1]
        n_tiles = xs_ref.shape[0] // EXPERT_TILE
        lax.fori_loop(0, N_EXPERTS, lambda e, c: fill(e, False), 0)
        lax.fori_loop(n_used, n_tiles, lambda i, c: fill_unused(i, False), 0)
        lax.fori_loop(0, N_EXPERTS, lambda e, c: fill(e, True), 0)
        lax.fori_loop(n_used, n_tiles, lambda i, c: fill_unused(i, True), 0)

    for t in range(tm):
        for slot in range(2):
            _row_copy(h_ref, t, xs_ref, pos_ref[slot, t], sem).start(priority=slot)
    for _ in range(2):
        pltpu.make_async_copy(h_ref, xs_ref.at[pl.ds(0, tm)], sem).wait()


def _dispatch(pos, tile_end, h1_flat, n_rows, *, tm):
    tp, d = h1_flat.shape
    return pl.pallas_call(
        functools.partial(_dispatch_kernel, tm=tm),
        grid=(tp // tm,),
        in_specs=[pl.BlockSpec((None, 2, tm), lambda i: (i, 0, 0), memory_space=pltpu.SMEM),
                  pl.BlockSpec(memory_space=pltpu.SMEM),
                  pl.BlockSpec((tm, d), lambda i: (i, 0))],
        out_specs=pl.BlockSpec(memory_space=pl.ANY),
        out_shape=jax.ShapeDtypeStruct((n_rows, d), F32),
        scratch_shapes=[pltpu.VMEM((EXPERT_TILE, d), F32), pltpu.SemaphoreType.DMA(())],
        compiler_params=_cparams(("arbitrary",)),
        name="moe_dispatch",
    )(pos, tile_end, h1_flat)


def _expert_kernel(te_ref, nused_ref, xs_ref, w1_ref, w3_ref, w2_ref, ys_ref):
    i = pl.program_id(0)

    @pl.when(i < nused_ref[0])
    def _():
        x = xs_ref[...].astype(BF16)
        a = _dot(x, w1_ref[...].astype(BF16))
        b = _dot(x, w3_ref[...].astype(BF16))
        hmid = (_silu(a) * b).astype(BF16)
        ys_ref[...] = _dot(hmid, w2_ref[...].astype(BF16))

    @pl.when(i >= nused_ref[0])
    def _():
        ys_ref[...] = jnp.zeros_like(ys_ref)


def _experts(tile_expert, n_used, xs, w1, w3, w2, layer):
    rows, d = xs.shape
    tme = EXPERT_TILE
    f = w1.shape[-1]
    grid_spec = pltpu.PrefetchScalarGridSpec(
        num_scalar_prefetch=2,
        grid=(rows // tme,),
        in_specs=[pl.BlockSpec((tme, d), lambda i, te, nu: (jnp.minimum(i, nu[0] - 1), 0)),
                  pl.BlockSpec((None, None, d, f), lambda i, te, nu: (layer, te[i], 0, 0)),
                  pl.BlockSpec((None, None, d, f), lambda i, te, nu: (layer, te[i], 0, 0)),
                  pl.BlockSpec((None, None, f, d), lambda i, te, nu: (layer, te[i], 0, 0))],
        out_specs=pl.BlockSpec((tme, d), lambda i, te, nu: (i, 0)),
    )
    return pl.pallas_call(
        _expert_kernel,
        grid_spec=grid_spec,
        out_shape=jax.ShapeDtypeStruct((rows, d), F32),
        compiler_params=_cparams(("arbitrary",)),
        name="moe_experts",
    )(tile_expert, n_used, xs, w1, w3, w2)


def _combine_kernel(pos_ref, route_ref, h1_ref, g_ref, b_ref, ys_ref,
                    o_ref, y0_ref, y1_ref, sem, *, tm, l, alpha):
    for t in range(tm):
        _row_copy(ys_ref, pos_ref[0, t], y0_ref, t, sem).start(priority=0)
        _row_copy(ys_ref, pos_ref[1, t], y1_ref, t, sem).start(priority=1)
    pltpu.make_async_copy(ys_ref.at[pl.ds(0, tm)], y0_ref, sem).wait()
    pltpu.make_async_copy(ys_ref.at[pl.ds(0, tm)], y1_ref, sem).wait()

    lane = lax.broadcasted_iota(jnp.int32, (1, LANES), 1)
    route = route_ref[...]
    gate1 = jnp.sum(jnp.where(lane == 4, route, 0.0), axis=-1, keepdims=True)
    gate2 = jnp.sum(jnp.where(lane == 5, route, 0.0), axis=-1, keepdims=True)
    ffn = gate1 * y0_ref[...] + gate2 * y1_ref[...]
    y = _layer_norm(alpha * h1_ref[...] + ffn, g_ref[...], b_ref[...])
    o_ref[...] = jnp.where(_row_valid(pl.program_id(1), tm, l), y, 0.0)


def _combine(pos, route, h1, g, b, ys, *, tm, l, alpha):
    bsz, lp, d = h1.shape
    nj = lp // tm
    row = lambda w: pl.BlockSpec((None, tm, w), lambda i, j: (i, j, 0))
    full = lambda s: pl.BlockSpec(s, lambda i, j: (0, 0))
    return pl.pallas_call(
        functools.partial(_combine_kernel, tm=tm, l=l, alpha=alpha),
        grid=(bsz, nj),
        in_specs=[pl.BlockSpec((None, 2, tm), lambda i, j: (i * nj + j, 0, 0), memory_space=pltpu.SMEM),
                  row(LANES), row(d), full((1, d)), full((1, d)), pl.BlockSpec(memory_space=pl.ANY)],
        out_specs=row(d),
        out_shape=jax.ShapeDtypeStruct((bsz, lp, d), F32),
        scratch_shapes=[pltpu.VMEM((tm, d), F32), pltpu.VMEM((tm, d), F32), pltpu.SemaphoreType.DMA(())],
        compiler_params=_cparams(("arbitrary", "arbitrary")),
        name="moe_combine_ln2",
    )(pos, route, h1, g, b, ys)


def kernel(x, meta_tokens, ln_in_g, ln_in_b, w_in, conv_w, a_log, dt_bias, sb_norm_g, gdn_norm_g, w_out,
           ln1_g, ln1_b, w_group, b_group, w_expert, b_expert, w1, w3, w2, ln2_g, ln2_b):
    bsz, seq, d = x.shape
    depth = w_in.shape[0]
    l = seq + N_META
    lp = -(-(FRONT + l) // LANES) * LANES
    tm = 384 if lp % 384 == 0 else LANES
    tp = bsz * lp
    alpha = float((2 * depth) ** 0.25)
    n_tiles = 2 * tp // EXPERT_TILE + N_EXPERTS
    row2 = lambda a: a.reshape(1, -1)

    meta = jnp.broadcast_to(meta_tokens.astype(x.dtype)[None], (bsz, N_META, d))
    xp = jnp.concatenate([jnp.zeros((bsz, FRONT, d), x.dtype), meta, x,
                          jnp.zeros((bsz, lp - FRONT - l, d), x.dtype)], axis=1)
    h = _ln_in(xp, row2(ln_in_g), row2(ln_in_b), tm=tm, l=l)

    n_main = 3 * SB_WIDTH + 4 * GDN_WIDTH
    for i in range(depth):
        w_small = jnp.pad(w_in[i, :, n_main:], ((0, 0), (0, LANES - 2 * GDN_HEADS)))
        q, k, v, gqkv, gz, gates = _inproj(h, w_in, w_small, i, tm=tm)
        o_sb = _sb_attention(q, k, v, row2(jnp.tile(sb_norm_g[i], 2)))
        o_gdn = _gdn(a_log[i], dt_bias[i], gqkv, gz, gates, conv_w[i], row2(gdn_norm_g[i]), tm=tm)

        w_router = jnp.pad(jnp.concatenate([w_group[i], w_expert[i]], axis=1),
                           ((0, 0), (0, LANES - N_GROUPS - N_EXPERTS)))
        b_router = jnp.pad(jnp.concatenate([b_group[i], b_expert[i]]), (0, LANES - N_GROUPS - N_EXPERTS))
        h1, route, counts = _route(o_sb, o_gdn, h, w_out[i], row2(ln1_g[i]), row2(ln1_b[i]),
                                   w_router, row2(b_router), tm=tm, l=l, alpha=alpha)

        ids = route[..., :4].astype(jnp.int32).reshape(tp // tm, tm, 4)
        cnt = counts[0, :N_EXPERTS].astype(jnp.int32)
        tiles = (cnt + EXPERT_TILE - 1) // EXPERT_TILE
        tile_end = jnp.cumsum(tiles)
        base = (tile_end - tiles) * EXPERT_TILE
        tile_expert = jnp.minimum(
            jnp.sum(jnp.arange(n_tiles, dtype=jnp.int32)[:, None] >= tile_end[None, :], axis=1),
            N_EXPERTS - 1).astype(jnp.int32)
        n_used = tile_end[-1:].astype(jnp.int32)
        pos = (base[ids[..., 0:2]] + ids[..., 2:4]).transpose(0, 2, 1).astype(jnp.int32)

        xs = _dispatch(pos, tile_end.astype(jnp.int32), h1.reshape(tp, d), n_tiles * EXPERT_TILE, tm=tm)
        ys = _experts(tile_expert, n_used, xs, w1, w3, w2, i)
        h = _combine(pos, route, h1, row2(ln2_g[i]), row2(ln2_b[i]), ys, tm=tm, l=l, alpha=alpha)

    return h[:, FRONT + N_META:FRONT + l]
```

```python
import functools

import jax
import jax.numpy as jnp
from jax import lax
from jax.experimental import pallas as pl
from jax.experimental.pallas import tpu as pltpu

N_META = 16
SB_HEADS = 8
SB_HEAD_DIM = 64
SB_WIDTH = SB_HEADS * SB_HEAD_DIM
GDN_HEADS = 4
GDN_HEAD_DIM = 128
GDN_WIDTH = GDN_HEADS * GDN_HEAD_DIM
GDN_CHUNK = 64
CONV_WIDTH = 4
N_GROUPS = 4
EXPERTS_PER_GROUP = 8
N_EXPERTS = N_GROUPS * EXPERTS_PER_GROUP
D_EXPERT = 256
LN_EPS = 1e-5
RMS_EPS = 1e-6

LANES = 128
SB_BLOCK = 128
FRONT = (-N_META) % GDN_CHUNK
EXPERT_TILE = 256
VMEM_LIMIT = 56 * 1024 * 1024
EXP_UNDERFLOW = -88.0
NEG_BIG = -1e30

F32 = jnp.float32
BF16 = jnp.bfloat16


def _cparams(sem):
    return pltpu.CompilerParams(dimension_semantics=sem, vmem_limit_bytes=VMEM_LIMIT)


def _dot(a, b):
    return jnp.dot(a, b, preferred_element_type=F32)


def _dot_nt(a, b):
    return lax.dot_general(a, b, (((1,), (1,)), ((), ())), preferred_element_type=F32)


def _dot_tn(a, b):
    return lax.dot_general(a, b, (((0,), (0,)), ((), ())), preferred_element_type=F32)


def _split3(x):
    hi = x.astype(BF16)
    r1 = x - hi.astype(F32)
    mid = r1.astype(BF16)
    lo = (r1 - mid.astype(F32)).astype(BF16)
    return hi, mid, lo


def _dot_sel(m01, x):
    hi, mid, lo = _split3(x)
    return _dot(m01, hi) + _dot(m01, mid) + _dot(m01, lo)


def _silu(x):
    return x / (1.0 + jnp.exp(-x))


def _softplus(x):
    return jnp.maximum(x, 0.0) + jnp.log1p(jnp.exp(-jnp.abs(x)))


def _layer_norm(x, g, b):
    mu = jnp.mean(x, axis=-1, keepdims=True)
    xc = x - mu
    var = jnp.mean(xc * xc, axis=-1, keepdims=True)
    return xc * lax.rsqrt(var + LN_EPS) * g + b


def _row_valid(j, tm, l):
    r = j * tm + lax.broadcasted_iota(jnp.int32, (tm, 1), 0)
    return (r >= FRONT) & (r < FRONT + l)


def _ln_in_kernel(x_ref, g_ref, b_ref, o_ref, *, tm, l):
    y = _layer_norm(x_ref[...], g_ref[...], b_ref[...])
    o_ref[...] = jnp.where(_row_valid(pl.program_id(1), tm, l), y, 0.0)


def _ln_in(xp, g, b, *, tm, l):
    bsz, lp, d = xp.shape
    return pl.pallas_call(
        functools.partial(_ln_in_kernel, tm=tm, l=l),
        grid=(bsz, lp // tm),
        in_specs=[pl.BlockSpec((None, tm, d), lambda i, j: (i, j, 0)),
                  pl.BlockSpec((1, d), lambda i, j: (0, 0)),
                  pl.BlockSpec((1, d), lambda i, j: (0, 0))],
        out_specs=pl.BlockSpec((None, tm, d), lambda i, j: (i, j, 0)),
        out_shape=jax.ShapeDtypeStruct((bsz, lp, d), F32),
        compiler_params=_cparams(("arbitrary", "arbitrary")),
        name="ln_in",
    )(xp, g, b)


def _inproj_kernel(x_ref, w_ref, ws_ref, q_ref, k_ref, v_ref, g_ref, z_ref, gt_ref, wb_ref, wsb_ref):
    first = (pl.program_id(0) == 0) & (pl.program_id(1) == 0)

    @pl.when(first)
    def _():
        wb_ref[...] = w_ref[...].astype(BF16)
        wsb_ref[...] = ws_ref[...].astype(BF16)

    x = x_ref[...].astype(BF16)
    s1, s2, s3 = SB_WIDTH, 2 * SB_WIDTH, 3 * SB_WIDTH
    s4 = s3 + 3 * GDN_WIDTH
    s5 = s4 + GDN_WIDTH
    q_ref[...] = _dot(x, wb_ref[:, 0:s1]).astype(BF16)
    k_ref[...] = _dot(x, wb_ref[:, s1:s2]).astype(BF16)
    v_ref[...] = _dot(x, wb_ref[:, s2:s3]).astype(BF16)
    g_ref[...] = _dot(x, wb_ref[:, s3:s4])
    z_ref[...] = _dot(x, wb_ref[:, s4:s5])
    gt_ref[...] = _dot(x, wsb_ref[...])


def _inproj(h, w_in, w_small, layer, *, tm):
    bsz, lp, d = h.shape
    n_main = 3 * SB_WIDTH + 4 * GDN_WIDTH
    row = lambda w: pl.BlockSpec((None, tm, w), lambda i, j: (i, j, 0))
    shp = lambda w, dt: jax.ShapeDtypeStruct((bsz, lp, w), dt)
    return pl.pallas_call(
        _inproj_kernel,
        grid=(bsz, lp // tm),
        in_specs=[row(d),
                  pl.BlockSpec((None, d, n_main), lambda i, j: (layer, 0, 0)),
                  pl.BlockSpec((d, LANES), lambda i, j: (0, 0))],
        out_specs=[row(SB_WIDTH), row(SB_WIDTH), row(SB_WIDTH), row(3 * GDN_WIDTH), row(GDN_WIDTH), row(LANES)],
        out_shape=[shp(SB_WIDTH, BF16), shp(SB_WIDTH, BF16), shp(SB_WIDTH, BF16),
                   shp(3 * GDN_WIDTH, F32), shp(GDN_WIDTH, F32), shp(LANES, F32)],
        scratch_shapes=[pltpu.VMEM((d, n_main), BF16), pltpu.VMEM((d, LANES), BF16)],
        compiler_params=_cparams(("arbitrary", "arbitrary")),
        name="inproj",
    )(h, w_in, w_small)


def _sb_scores(qs, kw, vw, vis, ucat, carry):
    n = len(qs)
    nk = kw[0].shape[0]
    z = [_dot_nt(qs[i], kw[i]) for i in range(n)]
    lnb = [-(jnp.maximum(z[i], 0.0) + jnp.log(1.0 + jnp.exp(-jnp.abs(z[i])))) for i in range(n)]
    if vis is not None:
        lnb = [jnp.where(vis[i], lnb[i], 0.0) for i in range(n)]
    hi = [x.astype(BF16) for x in lnb]
    lo = [(lnb[i] - hi[i].astype(F32)).astype(BF16) for i in range(n)]
    t = [_dot(jnp.concatenate([hi[i], lo[i]], axis=0), ucat) for i in range(n)]
    t = [x[:2 * SB_BLOCK] + x[2 * SB_BLOCK:] for x in t]
    p16 = []
    for i in range(n):
        between = t[i][:, :nk] - lnb[i]
        if carry is not None:
            between = between + carry[i]
        logw = z[i] + lnb[i] + between
        p = jnp.exp(logw) if vis is None else jnp.where(vis[i], jnp.exp(logw), 0.0)
        p16.append(p.astype(BF16))
    pv = [_dot(p16[i], vw[i]) for i in range(n)]
    return [(pv[i], t[i][:, nk:]) for i in range(n)]


def _sb_kernel(q_ref, k_ref, v_ref, g_ref, o_ref, *, qb):
    blk = SB_BLOCK
    win = 2 * blk
    step = pl.program_id(2)
    head0 = lax.broadcasted_iota(jnp.int32, (1, LANES), 1) < SB_HEAD_DIM
    zero = jnp.zeros((), BF16)

    def selector(n):
        r = lax.broadcasted_iota(jnp.int32, (n, n), 0)
        c = lax.broadcasted_iota(jnp.int32, (n, n), 1)
        return jnp.concatenate([(r >= c).astype(BF16), jnp.ones((n, LANES), BF16)], axis=1)

    ucat_win = selector(win)
    ucat_blk = selector(blk)
    row = lax.broadcasted_iota(jnp.int32, (blk, win), 0)
    col = lax.broadcasted_iota(jnp.int32, (blk, win), 1)

    def merge(pv):
        return jnp.where(head0, pv[:blk], pv[blk:])

    ibs, qss, kws, vws, viss = [], [], [], [], []
    for b in range(qb):
        ib = step * qb + b
        s0 = pl.multiple_of(jnp.maximum(ib - 1, 0) * blk, blk)
        q2 = q_ref[b * blk:(b + 1) * blk, :] * jnp.asarray(SB_HEAD_DIM ** -0.5, BF16)
        vis1 = (s0 + col) < (ib * blk + row)
        ibs.append(ib)
        qss.append(jnp.concatenate([jnp.where(head0, q2, zero), jnp.where(head0, zero, q2)], axis=0))
        kws.append(k_ref[pl.ds(s0, win), :])
        vws.append(v_ref[pl.ds(s0, win), :])
        viss.append(jnp.concatenate([vis1, vis1], axis=0))
    first = _sb_scores(qss, kws, vws, viss, ucat_win, None)

    for b, (pv0, tot) in enumerate(first):
        ib, qs, acc = ibs[b], qss[b], merge(pv0)

        def body(carry, qs=qs):
            j, c, a, _ = carry
            start = pl.multiple_of(j * blk, blk)
            (pv, t), = _sb_scores([qs], [k_ref[pl.ds(start, blk), :]], [v_ref[pl.ds(start, blk), :]],
                                  None, ucat_blk, [c])
            c = c + t
            return j - 1, c, a + merge(pv), jnp.max(c)

        def cond(carry):
            j, _, _, cmax = carry
            return (j >= 0) & (cmax >= EXP_UNDERFLOW)

        _, _, acc, _ = lax.while_loop(cond, body, (jnp.maximum(ib - 1, 0) - 1, tot, acc, jnp.max(tot)))

        sq = acc * acc
        ms0 = jnp.sum(jnp.where(head0, sq, 0.0), axis=-1, keepdims=True)
        ms1 = jnp.sum(jnp.where(head0, 0.0, sq), axis=-1, keepdims=True)
        ms = jnp.where(head0, ms0, ms1) * (1.0 / SB_HEAD_DIM)
        o_ref[b * blk:(b + 1) * blk, :] = acc * lax.rsqrt(ms + RMS_EPS) * g_ref[...]


def _sb_attention(q, k, v, g2):
    bsz, lp, _ = q.shape
    blk = SB_BLOCK
    nblk = lp // blk
    qb = next(c for c in (11, 3, 2, 1) if nblk % c == 0)
    npair = SB_WIDTH // LANES
    return pl.pallas_call(
        functools.partial(_sb_kernel, qb=qb),
        grid=(bsz, npair, nblk // qb),
        in_specs=[pl.BlockSpec((None, qb * blk, LANES), lambda b, p, i: (b, i, p)),
                  pl.BlockSpec((None, lp, LANES), lambda b, p, i: (b, 0, p)),
                  pl.BlockSpec((None, lp, LANES), lambda b, p, i: (b, 0, p)),
                  pl.BlockSpec((1, LANES), lambda b, p, i: (0, 0))],
        out_specs=pl.BlockSpec((None, qb * blk, LANES), lambda b, p, i: (b, i, p)),
        out_shape=jax.ShapeDtypeStruct((bsz, lp, SB_WIDTH), F32),
        compiler_params=_cparams(("arbitrary", "arbitrary", "arbitrary")),
        name="sb_attn",
    )(q, k, v, g2)


def _gdn_kernel(alog_ref, dtb_ref, x_ref, z_ref, gt_ref, cw_ref, gn_ref, o_ref,
                s_ref, halo_ref, qe_ref, oi_ref, xm_ref, nn_ref, cd_ref, *, tm):
    c_len = GDN_CHUNK
    hd = GDN_HEAD_DIM
    nch = tm // c_len
    j = pl.program_id(1)

    @pl.when(j == 0)
    def _():
        s_ref[...] = jnp.zeros_like(s_ref)
        halo_ref[...] = jnp.zeros_like(halo_ref)

    li = lax.broadcasted_iota(jnp.int32, (c_len, c_len), 0)
    lj = lax.broadcasted_iota(jnp.int32, (c_len, c_len), 1)
    lower_incl = li >= lj
    lower_strict = li > lj
    m_cum = jnp.concatenate([lower_incl.astype(BF16), jnp.ones((c_len, c_len), BF16)], axis=0)
    eye = (li == lj).astype(F32)
    lane = lax.broadcasted_iota(jnp.int32, (1, LANES), 1)

    gate_lane = (lane >= GDN_HEADS) & (lane < 2 * GDN_HEADS)
    alog_lane = jnp.zeros((1, LANES), F32)
    dtb_lane = jnp.zeros((1, LANES), F32)
    for h in range(GDN_HEADS):
        alog_lane = jnp.where(lane == GDN_HEADS + h, alog_ref[h], alog_lane)
        dtb_lane = jnp.where(lane == GDN_HEADS + h, dtb_ref[h], dtb_lane)
    rate_lane = jnp.exp(alog_lane)

    def column(x, idx):
        return jnp.sum(jnp.where(lane == idx, x, 0.0), axis=-1, keepdims=True)

    def prepare(chunks):
        per_chunk = []
        for c in chunks:
            if isinstance(c, int):
                r0 = c * c_len
                halo = halo_ref[...] if c == 0 else x_ref[r0 - 8:r0, :]
            else:
                r0 = pl.multiple_of(c * c_len, c_len)
                halo = x_ref[pl.ds(pl.multiple_of(r0 - 8, 8), 8), :]
            xe = jnp.concatenate([halo, x_ref[pl.ds(r0, c_len), :]], axis=0)
            conv = xe[8:] * cw_ref[CONV_WIDTH - 1:CONV_WIDTH, :]
            for tap in range(CONV_WIDTH - 1):
                shift = CONV_WIDTH - 1 - tap
                conv = conv + pltpu.roll(xe, shift, axis=0)[8:] * cw_ref[tap:tap + 1, :]
            y = _silu(conv)
            gt = gt_ref[pl.ds(r0, c_len), :]
            rows = j * tm + r0 + lax.broadcasted_iota(jnp.int32, (c_len, 1), 0)
            valid = rows >= FRONT
            beta_all = jnp.where(valid, 1.0 / (1.0 + jnp.exp(-gt)), 0.0)
            g_all = jnp.where(valid & gate_lane, -rate_lane * _softplus(gt + dtb_lane), 0.0)
            per_chunk.append((c, y, beta_all, g_all))

        gcats = [_dot_sel(m_cum, g_all) for (_, _, _, g_all) in per_chunk]
        gc_rows = [gcat[:c_len].T for gcat in gcats]

        probs = []
        for (c, y, beta_all, _), gcat, gc_row in zip(per_chunk, gcats, gc_rows):
            for h in range(GDN_HEADS):
                qr = y[:, h * hd:(h + 1) * hd]
                kr = y[:, GDN_WIDTH + h * hd:GDN_WIDTH + (h + 1) * hd]
                v = y[:, 2 * GDN_WIDTH + h * hd:2 * GDN_WIDTH + (h + 1) * hd]
                q = qr * lax.rsqrt(jnp.sum(qr * qr, axis=-1, keepdims=True) + RMS_EPS) * (hd ** -0.5)
                k = kr * lax.rsqrt(jnp.sum(kr * kr, axis=-1, keepdims=True) + RMS_EPS)
                beta = column(beta_all, h)
                gc = column(gcat[:c_len], GDN_HEADS + h)
                gtot = column(gcat[c_len:], GDN_HEADS + h)
                diff = gc - gc_row[GDN_HEADS + h:GDN_HEADS + h + 1, :]
                decay = jnp.where(lower_incl, jnp.exp(jnp.where(lower_incl, diff, 0.0)), 0.0)
                kb = k * beta
                probs.append(dict(
                    slot=c * GDN_HEADS + h, q=q, k=k, k16=k.astype(BF16), kb=kb, decay=decay,
                    rhs=jnp.concatenate([kb * jnp.exp(gc), v * beta], axis=1).astype(BF16),
                    qd=q * jnp.exp(gc), ke16=(k * jnp.exp(gtot - gc)).astype(BF16),
                    cd=jnp.exp(gtot[0:1, :])))

        for p in probs:
            kq = _dot_nt(jnp.concatenate([p["kb"], p["q"]], axis=0).astype(BF16), p["k16"])
            p["a"] = jnp.where(lower_strict, kq[:c_len] * p["decay"], 0.0)
            p["qk16"] = (kq[c_len:] * p["decay"]).astype(BF16)
            p["pw"] = p["a"]
            p["t"] = eye - p["a"]
        for _ in range(5):
            for p in probs:
                p16 = p["pw"].astype(BF16)
                p["pw"] = _dot(p16, p16)
            for p in probs:
                p["t"] = p["t"] + _dot(p["t"].astype(BF16), p["pw"].astype(BF16))
        for p in probs:
            p["wu"] = _dot(p["t"].astype(BF16), p["rhs"]).astype(BF16)
        for p in probs:
            p["qkwu"] = _dot(p["qk16"], p["wu"])
        for p in probs:
            p["kewu"] = _dot_tn(p["ke16"], p["wu"])
        for p in probs:
            slot = p["slot"]
            qe_ref[slot] = (p["qd"] - p["qkwu"][:, :hd]).astype(BF16)
            oi_ref[slot] = p["qkwu"][:, hd:]
            xm_ref[slot] = (-p["kewu"][:, :hd]).astype(BF16)
            nn_ref[slot] = p["kewu"][:, hd:]
            cd_ref[slot] = jnp.broadcast_to(p["cd"], (8, LANES))

    def scan(c):
        r0 = c * c_len if isinstance(c, int) else pl.multiple_of(c * c_len, c_len)
        heads = range(GDN_HEADS)
        slots = [c * GDN_HEADS + h for h in heads]
        s = [s_ref[h] for h in heads]
        s16 = [x.astype(BF16) for x in s]
        so = [_dot(qe_ref[slots[h]], s16[h]) for h in heads]
        sx = [_dot(xm_ref[slots[h]], s16[h]) for h in heads]
        for h in heads:
            s_ref[h] = s[h] * cd_ref[slots[h]][0:1, :] + sx[h] + nn_ref[slots[h]]
        for h in heads:
            o = so[h] + oi_ref[slots[h]]
            zh = z_ref[pl.ds(r0, c_len), h * hd:(h + 1) * hd]
            on = o * lax.rsqrt(jnp.mean(o * o, axis=-1, keepdims=True) + RMS_EPS) * gn_ref[...]
            o_ref[pl.ds(r0, c_len), h * hd:(h + 1) * hd] = on * _silu(zh)

    prepare([0, 1])

    def pipelined(it, _):
        scan(2 * it - 2)
        scan(2 * it - 1)
        prepare([2 * it, 2 * it + 1])
        return 0

    lax.fori_loop(1, nch // 2, pipelined, 0)
    scan(nch - 2)
    scan(nch - 1)
    halo_ref[...] = x_ref[tm - 8:tm, :]


def _gdn(a_log, dt_bias, gqkv, gz, gates, conv_w, gn, *, tm):
    bsz, lp, _ = gqkv.shape
    nslot = (tm // GDN_CHUNK) * GDN_HEADS
    row = lambda w: pl.BlockSpec((None, tm, w), lambda i, j: (i, j, 0))
    smem = pl.BlockSpec(memory_space=pltpu.SMEM)
    return pl.pallas_call(
        functools.partial(_gdn_kernel, tm=tm),
        grid=(bsz, lp // tm),
        in_specs=[smem, smem, row(3 * GDN_WIDTH), row(GDN_WIDTH), row(LANES),
                  pl.BlockSpec((CONV_WIDTH, 3 * GDN_WIDTH), lambda i, j: (0, 0)),
                  pl.BlockSpec((1, GDN_HEAD_DIM), lambda i, j: (0, 0))],
        out_specs=row(GDN_WIDTH),
        out_shape=jax.ShapeDtypeStruct((bsz, lp, GDN_WIDTH), F32),
        scratch_shapes=[pltpu.VMEM((GDN_HEADS, GDN_HEAD_DIM, GDN_HEAD_DIM), F32),
                        pltpu.VMEM((8, 3 * GDN_WIDTH), F32),
                        pltpu.VMEM((nslot, GDN_CHUNK, GDN_HEAD_DIM), BF16),
                        pltpu.VMEM((nslot, GDN_CHUNK, GDN_HEAD_DIM), F32),
                        pltpu.VMEM((nslot, GDN_HEAD_DIM, GDN_HEAD_DIM), BF16),
                        pltpu.VMEM((nslot, GDN_HEAD_DIM, GDN_HEAD_DIM), F32),
                        pltpu.VMEM((nslot, 8, LANES), F32)],
        compiler_params=_cparams(("arbitrary", "arbitrary")),
        name="gdn",
    )(a_log, dt_bias, gqkv, gz, gates, conv_w, gn)


def _route_kernel(osb_ref, ogdn_ref, h_ref, wo_ref, g_ref, b_ref, wr_ref, br_ref,
                  h1_ref, route_ref, cnt_ref, wob_ref, wrh_ref, wrl_ref, carry_ref, *, tm, l, alpha):
    first = (pl.program_id(0) == 0) & (pl.program_id(1) == 0)

    @pl.when(first)
    def _():
        wob_ref[...] = wo_ref[...].astype(BF16)
        wr_hi = wr_ref[...].astype(BF16)
        wrh_ref[...] = wr_hi
        wrl_ref[...] = (wr_ref[...] - wr_hi.astype(F32)).astype(BF16)
        carry_ref[...] = jnp.zeros_like(carry_ref)

    mix = (_dot(osb_ref[...].astype(BF16), wob_ref[0:SB_WIDTH, :])
           + _dot(ogdn_ref[...].astype(BF16), wob_ref[SB_WIDTH:, :]))
    y = _layer_norm(alpha * h_ref[...] + mix, g_ref[...], b_ref[...])
    h1 = jnp.where(_row_valid(pl.program_id(1), tm, l), y, 0.0)
    h1_ref[...] = h1

    h_hi = h1.astype(BF16)
    h_lo = (h1 - h_hi.astype(F32)).astype(BF16)
    lg = (_dot(h_hi, wrh_ref[...]) + _dot(h_lo, wrh_ref[...]) + _dot(h_hi, wrl_ref[...])) + br_ref[...]
    lane = lax.broadcasted_iota(jnp.int32, (1, LANES), 1).astype(F32)
    big = jnp.float32(1e9)
    gmask = lane < N_GROUPS
    gl = jnp.where(gmask, lg, NEG_BIG)
    gmax = jnp.max(gl, axis=-1, keepdims=True)
    gidx = jnp.min(jnp.where(gl == gmax, lane, big), axis=-1, keepdims=True)
    g_val = 1.0 / jnp.sum(jnp.where(gmask, jnp.exp(gl - gmax), 0.0), axis=-1, keepdims=True)
    lo = N_GROUPS + EXPERTS_PER_GROUP * gidx
    emask = (lane >= lo) & (lane < lo + EXPERTS_PER_GROUP)
    el = jnp.where(emask, lg, NEG_BIG)
    v1 = jnp.max(el, axis=-1, keepdims=True)
    i1 = jnp.min(jnp.where(el == v1, lane, big), axis=-1, keepdims=True)
    el2 = jnp.where(lane == i1, NEG_BIG, el)
    v2 = jnp.max(el2, axis=-1, keepdims=True)
    i2 = jnp.min(jnp.where(el2 == v2, lane, big), axis=-1, keepdims=True)
    e21 = jnp.exp(v2 - v1)
    w1 = 1.0 / (1.0 + e21)
    w2 = e21 * w1
    e1 = i1 - N_GROUPS
    e2 = i2 - N_GROUPS

    onehot = ((lane == e1) | (lane == e2)).astype(BF16)
    ri = lax.broadcasted_iota(jnp.int32, (tm, tm), 0)
    ci = lax.broadcasted_iota(jnp.int32, (tm, tm), 1)
    before = _dot((ri > ci).astype(BF16), onehot) + carry_ref[...]
    r1 = jnp.sum(jnp.where(lane == e1, before, 0.0), axis=-1, keepdims=True)
    r2 = jnp.sum(jnp.where(lane == e2, before, 0.0), axis=-1, keepdims=True)
    total = carry_ref[...] + jnp.sum(onehot.astype(F32), axis=0, keepdims=True)
    carry_ref[...] = total
    cnt_ref[...] = total

    out = jnp.zeros((tm, LANES), F32)
    for idx, val in enumerate((e1, e2, r1, r2, g_val * w1, g_val * w2)):
        out = jnp.where(lane == idx, val, out)
    route_ref[...] = out


def _route(o_sb, o_gdn, h, w_out, g, b, w_router, b_router, *, tm, l, alpha):
    bsz, lp, d = h.shape
    row = lambda w: pl.BlockSpec((None, tm, w), lambda i, j: (i, j, 0))
    full = lambda s: pl.BlockSpec(s, lambda i, j: (0, 0))
    return pl.pallas_call(
        functools.partial(_route_kernel, tm=tm, l=l, alpha=alpha),
        grid=(bsz, lp // tm),
        in_specs=[row(SB_WIDTH), row(GDN_WIDTH), row(d), full(w_out.shape), full((1, d)), full((1, d)),
                  full((d, LANES)), full((1, LANES))],
        out_specs=[row(d), row(LANES), full((1, LANES))],
        out_shape=[jax.ShapeDtypeStruct((bsz, lp, d), F32),
                   jax.ShapeDtypeStruct((bsz, lp, LANES), F32),
                   jax.ShapeDtypeStruct((1, LANES), F32)],
        scratch_shapes=[pltpu.VMEM(w_out.shape, BF16), pltpu.VMEM((d, LANES), BF16),
                        pltpu.VMEM((d, LANES), BF16), pltpu.VMEM((1, LANES), F32)],
        compiler_params=_cparams(("arbitrary", "arbitrary")),
        name="outproj_ln1_route",
    )(o_sb, o_gdn, h, w_out, g, b, w_router, b_router)


def _row_copy(src, s, dst, d, sem):
    return pltpu.make_async_copy(src.at[pl.ds(s, 1)], dst.at[pl.ds(d, 1)], sem)


def _dispatch_kernel(pos_ref, tile_end_ref, h_ref, xs_ref, zero_ref, sem, *, tm):

    @pl.when(pl.program_id(0) == 0)
    def _():
        zero_ref[...] = jnp.zeros_like(zero_ref)

        def fill(e, wait):
            first_tile = jnp.where(e == 0, 0, tile_end_ref[jnp.maximum(e - 1, 0)])

            @pl.when(tile_end_ref[e] > first_tile)
            def _():
                dst = pl.multiple_of((tile_end_ref[e] - 1) * EXPERT_TILE, EXPERT_TILE)
                cp = pltpu.make_async_copy(zero_ref, xs_ref.at[pl.ds(dst, EXPERT_TILE)], sem)
                if wait:
                    cp.wait()
                else:
                    cp.start()
            return 0

        def fill_unused(i, wait):
            dst = pl.multiple_of(i * EXPERT_TILE, EXPERT_TILE)
            cp = pltpu.make_async_copy(zero_ref, xs_ref.at[pl.ds(dst, EXPERT_TILE)], sem)
            cp.wait() if wait else cp.start()
            return 0

        n_used = tile_end_ref[N_EXPERTS - 1]
        n_tiles = xs_ref.shape[0] // EXPERT_TILE
        lax.fori_loop(0, N_EXPERTS, lambda e, c: fill(e, False), 0)
        lax.fori_loop(n_used, n_tiles, lambda i, c: fill_unused(i, False), 0)
        lax.fori_loop(0, N_EXPERTS, lambda e, c: fill(e, True), 0)
        lax.fori_loop(n_used, n_tiles, lambda i, c: fill_unused(i, True), 0)

    for t in range(tm):
        for slot in range(2):
            _row_copy(h_ref, t, xs_ref, pos_ref[slot, t], sem).start(priority=slot)
    for _ in range(2):
        pltpu.make_async_copy(h_ref, xs_ref.at[pl.ds(0, tm)], sem).wait()


def _dispatch(pos, tile_end, h1_flat, n_rows, *, tm):
    tp, d = h1_flat.shape
    return pl.pallas_call(
        functools.partial(_dispatch_kernel, tm=tm),
        grid=(tp // tm,),
        in_specs=[pl.BlockSpec((None, 2, tm), lambda i: (i, 0, 0), memory_space=pltpu.SMEM),
                  pl.BlockSpec(memory_space=pltpu.SMEM),
                  pl.BlockSpec((tm, d), lambda i: (i, 0))],
        out_specs=pl.BlockSpec(memory_space=pl.ANY),
        out_shape=jax.ShapeDtypeStruct((n_rows, d), h1_flat.dtype),
        scratch_shapes=[pltpu.VMEM((EXPERT_TILE, d), h1_flat.dtype), pltpu.SemaphoreType.DMA(())],
        compiler_params=_cparams(("arbitrary",)),
        name="moe_dispatch",
    )(pos, tile_end, h1_flat)


def _expert_kernel(te_ref, nused_ref, xs_ref, w1_ref, w3_ref, w2_ref, ys_ref):
    i = pl.program_id(0)

    @pl.when(i < nused_ref[0])
    def _():
        x = xs_ref[...].astype(BF16)
        a = _dot(x, w1_ref[...].astype(BF16))
        b = _dot(x, w3_ref[...].astype(BF16))
        hmid = (_silu(a) * b).astype(BF16)
        ys_ref[...] = _dot(hmid, w2_ref[...].astype(BF16))

    @pl.when(i >= nused_ref[0])
    def _():
        ys_ref[...] = jnp.zeros_like(ys_ref)


def _experts(tile_expert, n_used, xs, w1, w3, w2, layer):
    rows, dp = xs.shape
    tme = EXPERT_TILE
    d, f = w1.shape[-2:]
    grid_spec = pltpu.PrefetchScalarGridSpec(
        num_scalar_prefetch=2,
        grid=(rows // tme,),
        in_specs=[pl.BlockSpec((tme, dp), lambda i, te, nu: (jnp.minimum(i, nu[0] - 1), 0)),
                  pl.BlockSpec((None, None, d, f), lambda i, te, nu: (layer, te[i], 0, 0)),
                  pl.BlockSpec((None, None, d, f), lambda i, te, nu: (layer, te[i], 0, 0)),
                  pl.BlockSpec((None, None, f, d), lambda i, te, nu: (layer, te[i], 0, 0))],
        out_specs=pl.BlockSpec((tme, dp), lambda i, te, nu: (i, 0)),
    )
    return pl.pallas_call(
        _expert_kernel,
        grid_spec=grid_spec,
        out_shape=jax.ShapeDtypeStruct((rows, dp), F32),
        compiler_params=_cparams(("arbitrary",)),
        name="moe_experts",
    )(tile_expert, n_used, xs, w1, w3, w2)


def _combine_kernel(pos_ref, route_ref, h1_ref, g_ref, b_ref, ys_ref,
                    o_ref, y0_ref, y1_ref, sem, *, tm, l, alpha):
    for t in range(tm):
        _row_copy(ys_ref, pos_ref[0, t], y0_ref, t, sem).start(priority=0)
        _row_copy(ys_ref, pos_ref[1, t], y1_ref, t, sem).start(priority=1)
    pltpu.make_async_copy(ys_ref.at[pl.ds(0, tm)], y0_ref, sem).wait()
    pltpu.make_async_copy(ys_ref.at[pl.ds(0, tm)], y1_ref, sem).wait()

    lane = lax.broadcasted_iota(jnp.int32, (1, LANES), 1)
    route = route_ref[...]
    gate1 = jnp.sum(jnp.where(lane == 4, route, 0.0), axis=-1, keepdims=True)
    gate2 = jnp.sum(jnp.where(lane == 5, route, 0.0), axis=-1, keepdims=True)
    ffn = gate1 * y0_ref[...] + gate2 * y1_ref[...]
    y = _layer_norm(alpha * h1_ref[...] + ffn, g_ref[...], b_ref[...])
    o_ref[...] = jnp.where(_row_valid(pl.program_id(1), tm, l), y, 0.0)


def _combine(pos, route, h1, g, b, ys, *, tm, l, alpha):
    bsz, lp, d = h1.shape
    nj = lp // tm
    row = lambda w: pl.BlockSpec((None, tm, w), lambda i, j: (i, j, 0))
    full = lambda s: pl.BlockSpec(s, lambda i, j: (0, 0))
    return pl.pallas_call(
        functools.partial(_combine_kernel, tm=tm, l=l, alpha=alpha),
        grid=(bsz, nj),
        in_specs=[pl.BlockSpec((None, 2, tm), lambda i, j: (i * nj + j, 0, 0), memory_space=pltpu.SMEM),
                  row(LANES), row(d), full((1, d)), full((1, d)), pl.BlockSpec(memory_space=pl.ANY)],
        out_specs=row(d),
        out_shape=jax.ShapeDtypeStruct((bsz, lp, d), F32),
        scratch_shapes=[pltpu.VMEM((tm, d), F32), pltpu.VMEM((tm, d), F32), pltpu.SemaphoreType.DMA(())],
        compiler_params=_cparams(("arbitrary", "arbitrary")),
        name="moe_combine_ln2",
    )(pos, route, h1, g, b, ys)


def kernel(x, meta_tokens, ln_in_g, ln_in_b, w_in, conv_w, a_log, dt_bias, sb_norm_g, gdn_norm_g, w_out,
           ln1_g, ln1_b, w_group, b_group, w_expert, b_expert, w1, w3, w2, ln2_g, ln2_b):
    bsz, seq, d = x.shape
    depth = w_in.shape[0]
    l = seq + N_META
    lp = -(-(FRONT + l) // LANES) * LANES
    tm = 384 if lp % 384 == 0 else LANES
    tp = bsz * lp
    alpha = float((2 * depth) ** 0.25)
    n_tiles = 2 * tp // EXPERT_TILE + N_EXPERTS
    row2 = lambda a: a.reshape(1, -1)

    meta = jnp.broadcast_to(meta_tokens.astype(x.dtype)[None], (bsz, N_META, d))
    xp = jnp.concatenate([jnp.zeros((bsz, FRONT, d), x.dtype), meta, x,
                          jnp.zeros((bsz, lp - FRONT - l, d), x.dtype)], axis=1)
    h = _ln_in(xp, row2(ln_in_g), row2(ln_in_b), tm=tm, l=l)

    n_main = 3 * SB_WIDTH + 4 * GDN_WIDTH
    for i in range(depth):
        w_small = jnp.pad(w_in[i, :, n_main:], ((0, 0), (0, LANES - 2 * GDN_HEADS)))
        q, k, v, gqkv, gz, gates = _inproj(h, w_in, w_small, i, tm=tm)
        o_sb = _sb_attention(q, k, v, row2(jnp.tile(sb_norm_g[i], 2)))
        o_gdn = _gdn(a_log[i], dt_bias[i], gqkv, gz, gates, conv_w[i], row2(gdn_norm_g[i]), tm=tm)

        w_router = jnp.pad(jnp.concatenate([w_group[i], w_expert[i]], axis=1),
                           ((0, 0), (0, LANES - N_GROUPS - N_EXPERTS)))
        b_router = jnp.pad(jnp.concatenate([b_group[i], b_expert[i]]), (0, LANES - N_GROUPS - N_EXPERTS))
        h1, route, counts = _route(o_sb, o_gdn, h, w_out[i], row2(ln1_g[i]), row2(ln1_b[i]),
                                   w_router, row2(b_router), tm=tm, l=l, alpha=alpha)

        ids = route[..., :4].astype(jnp.int32).reshape(tp // tm, tm, 4)
        cnt = counts[0, :N_EXPERTS].astype(jnp.int32)
        tiles = (cnt + EXPERT_TILE - 1) // EXPERT_TILE
        tile_end = jnp.cumsum(tiles)
        base = (tile_end - tiles) * EXPERT_TILE
        tile_expert = jnp.minimum(
            jnp.sum(jnp.arange(n_tiles, dtype=jnp.int32)[:, None] >= tile_end[None, :], axis=1),
            N_EXPERTS - 1).astype(jnp.int32)
        n_used = tile_end[-1:].astype(jnp.int32)
        is_expert = ids[..., 0:2, None] == jnp.arange(N_EXPERTS, dtype=jnp.int32)
        seg_base = jnp.sum(jnp.where(is_expert, base, 0), axis=-1)
        pos = (seg_base + ids[..., 2:4]).transpose(0, 2, 1).astype(jnp.int32)

        xs = _dispatch(pos, tile_end.astype(jnp.int32), h1.reshape(tp, d), n_tiles * EXPERT_TILE, tm=tm)
        ys = _experts(tile_expert, n_used, xs, w1, w3, w2, i)
        h = _combine(pos, route, h1, row2(ln2_g[i]), row2(ln2_b[i]), ys, tm=tm, l=l, alpha=alpha)

    return h[:, FRONT + N_META:FRONT + l]
```

```python
import functools

import jax
import jax.numpy as jnp
from jax import lax
from jax.experimental import pallas as pl
from jax.experimental.pallas import tpu as pltpu

N_META = 16
SB_HEADS = 8
SB_HEAD_DIM = 64
SB_WIDTH = SB_HEADS * SB_HEAD_DIM
GDN_HEADS = 4
GDN_HEAD_DIM = 128
GDN_WIDTH = GDN_HEADS * GDN_HEAD_DIM
GDN_CHUNK = 64
CONV_WIDTH = 4
N_GROUPS = 4
EXPERTS_PER_GROUP = 8
N_EXPERTS = N_GROUPS * EXPERTS_PER_GROUP
D_EXPERT = 256
LN_EPS = 1e-5
RMS_EPS = 1e-6

LANES = 128
SB_BLOCK = 128
FRONT = (-N_META) % GDN_CHUNK
EXPERT_TILE = 256
VMEM_LIMIT = 56 * 1024 * 1024
EXP_UNDERFLOW = -88.0
NEG_BIG = -1e30

F32 = jnp.float32
BF16 = jnp.bfloat16


def _cparams(sem):
    return pltpu.CompilerParams(dimension_semantics=sem, vmem_limit_bytes=VMEM_LIMIT)


def _dot(a, b):
    return jnp.dot(a, b, preferred_element_type=F32)


def _dot_nt(a, b):
    return lax.dot_general(a, b, (((1,), (1,)), ((), ())), preferred_element_type=F32)


def _dot_tn(a, b):
    return lax.dot_general(a, b, (((0,), (0,)), ((), ())), preferred_element_type=F32)


def _split3(x):
    hi = x.astype(BF16)
    r1 = x - hi.astype(F32)
    mid = r1.astype(BF16)
    lo = (r1 - mid.astype(F32)).astype(BF16)
    return hi, mid, lo


def _dot_sel(m01, x):
    hi, mid, lo = _split3(x)
    return _dot(m01, hi) + _dot(m01, mid) + _dot(m01, lo)


def _silu(x):
    return x / (1.0 + jnp.exp(-x))


def _softplus(x):
    return jnp.maximum(x, 0.0) + jnp.log1p(jnp.exp(-jnp.abs(x)))


def _layer_norm(x, g, b):
    mu = jnp.mean(x, axis=-1, keepdims=True)
    xc = x - mu
    var = jnp.mean(xc * xc, axis=-1, keepdims=True)
    return xc * lax.rsqrt(var + LN_EPS) * g + b


def _row_valid(j, tm, l):
    r = j * tm + lax.broadcasted_iota(jnp.int32, (tm, 1), 0)
    return (r >= FRONT) & (r < FRONT + l)


def _ln_in_kernel(x_ref, g_ref, b_ref, o_ref, *, tm, l):
    y = _layer_norm(x_ref[...], g_ref[...], b_ref[...])
    o_ref[...] = jnp.where(_row_valid(pl.program_id(1), tm, l), y, 0.0)


def _ln_in(xp, g, b, *, tm, l):
    bsz, lp, d = xp.shape
    return pl.pallas_call(
        functools.partial(_ln_in_kernel, tm=tm, l=l),
        grid=(bsz, lp // tm),
        in_specs=[pl.BlockSpec((None, tm, d), lambda i, j: (i, j, 0)),
                  pl.BlockSpec((1, d), lambda i, j: (0, 0)),
                  pl.BlockSpec((1, d), lambda i, j: (0, 0))],
        out_specs=pl.BlockSpec((None, tm, d), lambda i, j: (i, j, 0)),
        out_shape=jax.ShapeDtypeStruct((bsz, lp, d), F32),
        compiler_params=_cparams(("arbitrary", "arbitrary")),
        name="ln_in",
    )(xp, g, b)


def _inproj_kernel(x_ref, w_ref, ws_ref, q_ref, k_ref, v_ref, g_ref, z_ref, gt_ref, wb_ref, wsb_ref):
    first = (pl.program_id(0) == 0) & (pl.program_id(1) == 0)

    @pl.when(first)
    def _():
        wb_ref[...] = w_ref[...].astype(BF16)
        wsb_ref[...] = ws_ref[...].astype(BF16)

    x = x_ref[...].astype(BF16)
    s1, s2, s3 = SB_WIDTH, 2 * SB_WIDTH, 3 * SB_WIDTH
    s4 = s3 + 3 * GDN_WIDTH
    s5 = s4 + GDN_WIDTH
    q_ref[...] = _dot(x, wb_ref[:, 0:s1]).astype(BF16)
    k_ref[...] = _dot(x, wb_ref[:, s1:s2]).astype(BF16)
    v_ref[...] = _dot(x, wb_ref[:, s2:s3]).astype(BF16)
    g_ref[...] = _dot(x, wb_ref[:, s3:s4])
    z_ref[...] = _dot(x, wb_ref[:, s4:s5])
    gt_ref[...] = _dot(x, wsb_ref[...])


def _inproj(h, w_in, w_small, layer, *, tm):
    bsz, lp, d = h.shape
    n_main = 3 * SB_WIDTH + 4 * GDN_WIDTH
    row = lambda w: pl.BlockSpec((None, tm, w), lambda i, j: (i, j, 0))
    shp = lambda w, dt: jax.ShapeDtypeStruct((bsz, lp, w), dt)
    return pl.pallas_call(
        _inproj_kernel,
        grid=(bsz, lp // tm),
        in_specs=[row(d),
                  pl.BlockSpec((None, d, n_main), lambda i, j: (layer, 0, 0)),
                  pl.BlockSpec((d, LANES), lambda i, j: (0, 0))],
        out_specs=[row(SB_WIDTH), row(SB_WIDTH), row(SB_WIDTH), row(3 * GDN_WIDTH), row(GDN_WIDTH), row(LANES)],
        out_shape=[shp(SB_WIDTH, BF16), shp(SB_WIDTH, BF16), shp(SB_WIDTH, BF16),
                   shp(3 * GDN_WIDTH, F32), shp(GDN_WIDTH, F32), shp(LANES, F32)],
        scratch_shapes=[pltpu.VMEM((d, n_main), BF16), pltpu.VMEM((d, LANES), BF16)],
        compiler_params=_cparams(("arbitrary", "arbitrary")),
        name="inproj",
    )(h, w_in, w_small)


def _sb_scores(qs, kw, vw, vis, ucat, carry):
    n = len(qs)
    nk = kw[0].shape[0]
    z = [_dot_nt(qs[i], kw[i]) for i in range(n)]
    if vis is not None:
        z = [jnp.where(vis[i], z[i], NEG_BIG) for i in range(n)]
    lnb = [-(jnp.maximum(z[i], 0.0) + jnp.log(1.0 + jnp.exp(-jnp.abs(z[i])))) for i in range(n)]
    hi = [x.astype(BF16) for x in lnb]
    lo = [(lnb[i] - hi[i].astype(F32)).astype(BF16) for i in range(n)]
    t = [_dot(jnp.concatenate([hi[i], lo[i]], axis=0), ucat) for i in range(n)]
    t = [x[:2 * SB_BLOCK] + x[2 * SB_BLOCK:] for x in t]
    p16 = []
    for i in range(n):
        between = t[i][:, :nk] - lnb[i]
        if carry is not None:
            between = between + carry[i]
        logw = z[i] + lnb[i] + between
        p16.append(jnp.exp(logw).astype(BF16))
    pv = [_dot(p16[i], vw[i]) for i in range(n)]
    return [(pv[i], t[i][:, nk:]) for i in range(n)]


def _sb_kernel(q_ref, k_ref, v_ref, g_ref, o_ref, *, qb):
    blk = SB_BLOCK
    win = 2 * blk
    step = pl.program_id(2)
    head0 = lax.broadcasted_iota(jnp.int32, (1, LANES), 1) < SB_HEAD_DIM
    zero = jnp.zeros((), BF16)

    def selector(n):
        r = lax.broadcasted_iota(jnp.int32, (n, n), 0)
        c = lax.broadcasted_iota(jnp.int32, (n, n), 1)
        return jnp.concatenate([(r >= c).astype(BF16), jnp.ones((n, LANES), BF16)], axis=1)

    ucat_win = selector(win)
    ucat_blk = selector(blk)
    row = lax.broadcasted_iota(jnp.int32, (blk, win), 0)
    col = lax.broadcasted_iota(jnp.int32, (blk, win), 1)

    def merge(pv):
        return jnp.where(head0, pv[:blk], pv[blk:])

    ibs, qss, kws, vws, viss = [], [], [], [], []
    for b in range(qb):
        ib = step * qb + b
        s0 = pl.multiple_of(jnp.maximum(ib - 1, 0) * blk, blk)
        q2 = q_ref[b * blk:(b + 1) * blk, :] * jnp.asarray(SB_HEAD_DIM ** -0.5, BF16)
        vis1 = (s0 + col) < (ib * blk + row)
        ibs.append(ib)
        qss.append(jnp.concatenate([jnp.where(head0, q2, zero), jnp.where(head0, zero, q2)], axis=0))
        kws.append(k_ref[pl.ds(s0, win), :])
        vws.append(v_ref[pl.ds(s0, win), :])
        viss.append(jnp.concatenate([vis1, vis1], axis=0))
    first = _sb_scores(qss, kws, vws, viss, ucat_win, None)

    for b, (pv0, tot) in enumerate(first):
        ib, qs, acc = ibs[b], qss[b], merge(pv0)

        def body(carry, qs=qs):
            j, c, a, _ = carry
            start = pl.multiple_of(j * blk, blk)
            (pv, t), = _sb_scores([qs], [k_ref[pl.ds(start, blk), :]], [v_ref[pl.ds(start, blk), :]],
                                  None, ucat_blk, [c])
            c = c + t
            return j - 1, c, a + merge(pv), jnp.max(c)

        def cond(carry):
            j, _, _, cmax = carry
            return (j >= 0) & (cmax >= EXP_UNDERFLOW)

        _, _, acc, _ = lax.while_loop(cond, body, (jnp.maximum(ib - 1, 0) - 1, tot, acc, jnp.max(tot)))

        sq = acc * acc
        ms0 = jnp.sum(jnp.where(head0, sq, 0.0), axis=-1, keepdims=True)
        ms1 = jnp.sum(jnp.where(head0, 0.0, sq), axis=-1, keepdims=True)
        ms = jnp.where(head0, ms0, ms1) * (1.0 / SB_HEAD_DIM)
        o_ref[b * blk:(b + 1) * blk, :] = acc * lax.rsqrt(ms + RMS_EPS) * g_ref[...]


def _sb_attention(q, k, v, g2):
    bsz, lp, _ = q.shape
    blk = SB_BLOCK
    nblk = lp // blk
    qb = next(c for c in (11, 3, 2, 1) if nblk % c == 0)
    npair = SB_WIDTH // LANES
    return pl.pallas_call(
        functools.partial(_sb_kernel, qb=qb),
        grid=(bsz, npair, nblk // qb),
        in_specs=[pl.BlockSpec((None, qb * blk, LANES), lambda b, p, i: (b, i, p)),
                  pl.BlockSpec((None, lp, LANES), lambda b, p, i: (b, 0, p)),
                  pl.BlockSpec((None, lp, LANES), lambda b, p, i: (b, 0, p)),
                  pl.BlockSpec((1, LANES), lambda b, p, i: (0, 0))],
        out_specs=pl.BlockSpec((None, qb * blk, LANES), lambda b, p, i: (b, i, p)),
        out_shape=jax.ShapeDtypeStruct((bsz, lp, SB_WIDTH), F32),
        compiler_params=_cparams(("arbitrary", "arbitrary", "arbitrary")),
        name="sb_attn",
    )(q, k, v, g2)


def _gdn_kernel(alog_ref, dtb_ref, x_ref, z_ref, gt_ref, cw_ref, gn_ref, o_ref,
                s_ref, halo_ref, qe_ref, oi_ref, xm_ref, nn_ref, cd_ref, *, tm):
    c_len = GDN_CHUNK
    hd = GDN_HEAD_DIM
    nch = tm // c_len
    j = pl.program_id(1)

    @pl.when(j == 0)
    def _():
        s_ref[...] = jnp.zeros_like(s_ref)
        halo_ref[...] = jnp.zeros_like(halo_ref)

    li = lax.broadcasted_iota(jnp.int32, (c_len, c_len), 0)
    lj = lax.broadcasted_iota(jnp.int32, (c_len, c_len), 1)
    lower_incl = li >= lj
    lower_strict = li > lj
    m_cum = jnp.concatenate([lower_incl.astype(BF16), jnp.ones((c_len, c_len), BF16)], axis=0)
    eye = (li == lj).astype(F32)
    lane = lax.broadcasted_iota(jnp.int32, (1, LANES), 1)

    gate_lane = (lane >= GDN_HEADS) & (lane < 2 * GDN_HEADS)
    alog_lane = jnp.zeros((1, LANES), F32)
    dtb_lane = jnp.zeros((1, LANES), F32)
    for h in range(GDN_HEADS):
        alog_lane = jnp.where(lane == GDN_HEADS + h, alog_ref[h], alog_lane)
        dtb_lane = jnp.where(lane == GDN_HEADS + h, dtb_ref[h], dtb_lane)
    rate_lane = jnp.exp(alog_lane)

    def column(x, idx):
        return jnp.sum(jnp.where(lane == idx, x, 0.0), axis=-1, keepdims=True)

    def prepare(chunks):
        per_chunk = []
        for c in chunks:
            if isinstance(c, int):
                r0 = c * c_len
                halo = halo_ref[...] if c == 0 else x_ref[r0 - 8:r0, :]
            else:
                r0 = pl.multiple_of(c * c_len, c_len)
                halo = x_ref[pl.ds(pl.multiple_of(r0 - 8, 8), 8), :]
            xe = jnp.concatenate([halo, x_ref[pl.ds(r0, c_len), :]], axis=0)
            conv = xe[8:] * cw_ref[CONV_WIDTH - 1:CONV_WIDTH, :]
            for tap in range(CONV_WIDTH - 1):
                shift = CONV_WIDTH - 1 - tap
                conv = conv + pltpu.roll(xe, shift, axis=0)[8:] * cw_ref[tap:tap + 1, :]
            y = _silu(conv)
            gt = gt_ref[pl.ds(r0, c_len), :]
            rows = j * tm + r0 + lax.broadcasted_iota(jnp.int32, (c_len, 1), 0)
            valid = rows >= FRONT
            beta_all = jnp.where(valid, 1.0 / (1.0 + jnp.exp(-gt)), 0.0)
            g_all = jnp.where(valid & gate_lane, -rate_lane * _softplus(gt + dtb_lane), 0.0)
            per_chunk.append((c, y, beta_all, g_all))

        gcats = [_dot_sel(m_cum, g_all) for (_, _, _, g_all) in per_chunk]
        gc_rows = [gcat[:c_len].T for gcat in gcats]

        probs = []
        for (c, y, beta_all, _), gcat, gc_row in zip(per_chunk, gcats, gc_rows):
            for h in range(GDN_HEADS):
                qr = y[:, h * hd:(h + 1) * hd]
                kr = y[:, GDN_WIDTH + h * hd:GDN_WIDTH + (h + 1) * hd]
                v = y[:, 2 * GDN_WIDTH + h * hd:2 * GDN_WIDTH + (h + 1) * hd]
                q = qr * lax.rsqrt(jnp.sum(qr * qr, axis=-1, keepdims=True) + RMS_EPS) * (hd ** -0.5)
                k = kr * lax.rsqrt(jnp.sum(kr * kr, axis=-1, keepdims=True) + RMS_EPS)
                beta = column(beta_all, h)
                gc = column(gcat[:c_len], GDN_HEADS + h)
                gtot = column(gcat[c_len:], GDN_HEADS + h)
                diff = gc - gc_row[GDN_HEADS + h:GDN_HEADS + h + 1, :]
                decay = jnp.where(lower_incl, jnp.exp(jnp.where(lower_incl, diff, 0.0)), 0.0)
                kb = k * beta
                probs.append(dict(
                    slot=c * GDN_HEADS + h, q=q, k=k, k16=k.astype(BF16), kb=kb, decay=decay,
                    rhs=jnp.concatenate([kb * jnp.exp(gc), v * beta], axis=1).astype(BF16),
                    qd=q * jnp.exp(gc), ke16=(k * jnp.exp(gtot - gc)).astype(BF16),
                    cd=jnp.exp(gtot[0:1, :])))

        for p in probs:
            kq = _dot_nt(jnp.concatenate([p["kb"], p["q"]], axis=0).astype(BF16), p["k16"])
            p["a"] = jnp.where(lower_strict, kq[:c_len] * p["decay"], 0.0)
            p["qk16"] = (kq[c_len:] * p["decay"]).astype(BF16)
            p["pw"] = p["a"]
            p["t"] = eye - p["a"]
        for _ in range(5):
            for p in probs:
                p16 = p["pw"].astype(BF16)
                p["pw"] = _dot(p16, p16)
            for p in probs:
                p["t"] = p["t"] + _dot(p["t"].astype(BF16), p["pw"].astype(BF16))
        for p in probs:
            p["wu"] = _dot(p["t"].astype(BF16), p["rhs"]).astype(BF16)
        for p in probs:
            p["qkwu"] = _dot(p["qk16"], p["wu"])
        for p in probs:
            p["kewu"] = _dot_tn(p["ke16"], p["wu"])
        for p in probs:
            slot = p["slot"]
            qe_ref[slot] = (p["qd"] - p["qkwu"][:, :hd]).astype(BF16)
            oi_ref[slot] = p["qkwu"][:, hd:]
            xm_ref[slot] = (-p["kewu"][:, :hd]).astype(BF16)
            nn_ref[slot] = p["kewu"][:, hd:]
            cd_ref[slot] = jnp.broadcast_to(p["cd"], (8, LANES))

    def scan(c):
        r0 = c * c_len if isinstance(c, int) else pl.multiple_of(c * c_len, c_len)
        heads = range(GDN_HEADS)
        slots = [c * GDN_HEADS + h for h in heads]
        s = [s_ref[h] for h in heads]
        s16 = [x.astype(BF16) for x in s]
        so = [_dot(qe_ref[slots[h]], s16[h]) for h in heads]
        sx = [_dot(xm_ref[slots[h]], s16[h]) for h in heads]
        for h in heads:
            s_ref[h] = s[h] * cd_ref[slots[h]][0:1, :] + sx[h] + nn_ref[slots[h]]
        for h in heads:
            o = so[h] + oi_ref[slots[h]]
            zh = z_ref[pl.ds(r0, c_len), h * hd:(h + 1) * hd]
            on = o * lax.rsqrt(jnp.mean(o * o, axis=-1, keepdims=True) + RMS_EPS) * gn_ref[...]
            o_ref[pl.ds(r0, c_len), h * hd:(h + 1) * hd] = on * _silu(zh)

    grp = 3 if nch % 3 == 0 else 2
    prepare(list(range(grp)))

    def pipelined(it, _):
        for g in range(grp):
            scan((it - 1) * grp + g)
        prepare([it * grp + g for g in range(grp)])
        return 0

    lax.fori_loop(1, nch // grp, pipelined, 0)
    for g in range(grp):
        scan(nch - grp + g)
    halo_ref[...] = x_ref[tm - 8:tm, :]


def _gdn(a_log, dt_bias, gqkv, gz, gates, conv_w, gn, *, tm):
    bsz, lp, _ = gqkv.shape
    nslot = (tm // GDN_CHUNK) * GDN_HEADS
    row = lambda w: pl.BlockSpec((None, tm, w), lambda i, j: (i, j, 0))
    smem = pl.BlockSpec(memory_space=pltpu.SMEM)
    return pl.pallas_call(
        functools.partial(_gdn_kernel, tm=tm),
        grid=(bsz, lp // tm),
        in_specs=[smem, smem, row(3 * GDN_WIDTH), row(GDN_WIDTH), row(LANES),
                  pl.BlockSpec((CONV_WIDTH, 3 * GDN_WIDTH), lambda i, j: (0, 0)),
                  pl.BlockSpec((1, GDN_HEAD_DIM), lambda i, j: (0, 0))],
        out_specs=row(GDN_WIDTH),
        out_shape=jax.ShapeDtypeStruct((bsz, lp, GDN_WIDTH), F32),
        scratch_shapes=[pltpu.VMEM((GDN_HEADS, GDN_HEAD_DIM, GDN_HEAD_DIM), F32),
                        pltpu.VMEM((8, 3 * GDN_WIDTH), F32),
                        pltpu.VMEM((nslot, GDN_CHUNK, GDN_HEAD_DIM), BF16),
                        pltpu.VMEM((nslot, GDN_CHUNK, GDN_HEAD_DIM), F32),
                        pltpu.VMEM((nslot, GDN_HEAD_DIM, GDN_HEAD_DIM), BF16),
                        pltpu.VMEM((nslot, GDN_HEAD_DIM, GDN_HEAD_DIM), F32),
                        pltpu.VMEM((nslot, 8, LANES), F32)],
        compiler_params=_cparams(("arbitrary", "arbitrary")),
        name="gdn",
    )(a_log, dt_bias, gqkv, gz, gates, conv_w, gn)


def _route_kernel(osb_ref, ogdn_ref, h_ref, wo_ref, g_ref, b_ref, wr_ref, br_ref,
                  h1_ref, route_ref, cnt_ref, wob_ref, wrh_ref, wrl_ref, carry_ref, *, tm, l, alpha):
    first = (pl.program_id(0) == 0) & (pl.program_id(1) == 0)

    @pl.when(first)
    def _():
        wob_ref[...] = wo_ref[...].astype(BF16)
        wr_hi = wr_ref[...].astype(BF16)
        wrh_ref[...] = wr_hi
        wrl_ref[...] = (wr_ref[...] - wr_hi.astype(F32)).astype(BF16)
        carry_ref[...] = jnp.zeros_like(carry_ref)

    mix = (_dot(osb_ref[...].astype(BF16), wob_ref[0:SB_WIDTH, :])
           + _dot(ogdn_ref[...].astype(BF16), wob_ref[SB_WIDTH:, :]))
    y = _layer_norm(alpha * h_ref[...] + mix, g_ref[...], b_ref[...])
    h1 = jnp.where(_row_valid(pl.program_id(1), tm, l), y, 0.0)
    h1_ref[...] = h1

    h_hi = h1.astype(BF16)
    h_lo = (h1 - h_hi.astype(F32)).astype(BF16)
    lg = (_dot(h_hi, wrh_ref[...]) + _dot(h_lo, wrh_ref[...]) + _dot(h_hi, wrl_ref[...])) + br_ref[...]
    lane = lax.broadcasted_iota(jnp.int32, (1, LANES), 1).astype(F32)
    big = jnp.float32(1e9)
    gmask = lane < N_GROUPS
    gl = jnp.where(gmask, lg, NEG_BIG)
    gmax = jnp.max(gl, axis=-1, keepdims=True)
    gidx = jnp.min(jnp.where(gl == gmax, lane, big), axis=-1, keepdims=True)
    g_val = 1.0 / jnp.sum(jnp.where(gmask, jnp.exp(gl - gmax), 0.0), axis=-1, keepdims=True)
    lo = N_GROUPS + EXPERTS_PER_GROUP * gidx
    emask = (lane >= lo) & (lane < lo + EXPERTS_PER_GROUP)
    el = jnp.where(emask, lg, NEG_BIG)
    v1 = jnp.max(el, axis=-1, keepdims=True)
    i1 = jnp.min(jnp.where(el == v1, lane, big), axis=-1, keepdims=True)
    el2 = jnp.where(lane == i1, NEG_BIG, el)
    v2 = jnp.max(el2, axis=-1, keepdims=True)
    i2 = jnp.min(jnp.where(el2 == v2, lane, big), axis=-1, keepdims=True)
    e21 = jnp.exp(v2 - v1)
    w1 = 1.0 / (1.0 + e21)
    w2 = e21 * w1
    e1 = i1 - N_GROUPS
    e2 = i2 - N_GROUPS

    onehot = ((lane == e1) | (lane == e2)).astype(BF16)
    ri = lax.broadcasted_iota(jnp.int32, (tm, tm), 0)
    ci = lax.broadcasted_iota(jnp.int32, (tm, tm), 1)
    before = _dot((ri > ci).astype(BF16), onehot) + carry_ref[...]
    r1 = jnp.sum(jnp.where(lane == e1, before, 0.0), axis=-1, keepdims=True)
    r2 = jnp.sum(jnp.where(lane == e2, before, 0.0), axis=-1, keepdims=True)
    total = carry_ref[...] + jnp.sum(onehot.astype(F32), axis=0, keepdims=True)
    carry_ref[...] = total
    cnt_ref[...] = total

    out = jnp.zeros((tm, LANES), F32)
    for idx, val in enumerate((e1, e2, r1, r2, g_val * w1, g_val * w2)):
        out = jnp.where(lane == idx, val, out)
    route_ref[...] = out


def _route(o_sb, o_gdn, h, w_out, g, b, w_router, b_router, *, tm, l, alpha):
    bsz, lp, d = h.shape
    row = lambda w: pl.BlockSpec((None, tm, w), lambda i, j: (i, j, 0))
    full = lambda s: pl.BlockSpec(s, lambda i, j: (0, 0))
    return pl.pallas_call(
        functools.partial(_route_kernel, tm=tm, l=l, alpha=alpha),
        grid=(bsz, lp // tm),
        in_specs=[row(SB_WIDTH), row(GDN_WIDTH), row(d), full(w_out.shape), full((1, d)), full((1, d)),
                  full((d, LANES)), full((1, LANES))],
        out_specs=[row(d), row(LANES), full((1, LANES))],
        out_shape=[jax.ShapeDtypeStruct((bsz, lp, d), F32),
                   jax.ShapeDtypeStruct((bsz, lp, LANES), F32),
                   jax.ShapeDtypeStruct((1, LANES), F32)],
        scratch_shapes=[pltpu.VMEM(w_out.shape, BF16), pltpu.VMEM((d, LANES), BF16),
                        pltpu.VMEM((d, LANES), BF16), pltpu.VMEM((1, LANES), F32)],
        compiler_params=_cparams(("arbitrary", "arbitrary")),
        name="outproj_ln1_route",
    )(o_sb, o_gdn, h, w_out, g, b, w_router, b_router)


def _row_copy(src, s, dst, d, sem):
    return pltpu.make_async_copy(src.at[pl.ds(s, 1)], dst.at[pl.ds(d, 1)], sem)


def _dispatch_kernel(pos_ref, tile_end_ref, h_ref, xs_ref, zero_ref, sem, *, tm):

    @pl.when(pl.program_id(0) == 0)
    def _():
        zero_ref[...] = jnp.zeros_like(zero_ref)

        def fill(e, wait):
            first_tile = jnp.where(e == 0, 0, tile_end_ref[jnp.maximum(e - 1, 0)])

            @pl.when(tile_end_ref[e] > first_tile)
            def _():
                dst = pl.multiple_of((tile_end_ref[e] - 1) * EXPERT_TILE, EXPERT_TILE)
                cp = pltpu.make_async_copy(zero_ref, xs_ref.at[pl.ds(dst, EXPERT_TILE)], sem)
                if wait:
                    cp.wait()
                else:
                    cp.start()
            return 0

        def fill_unused(i, wait):
            dst = pl.multiple_of(i * EXPERT_TILE, EXPERT_TILE)
            cp = pltpu.make_async_copy(zero_ref, xs_ref.at[pl.ds(dst, EXPERT_TILE)], sem)
            cp.wait() if wait else cp.start()
            return 0

        n_used = tile_end_ref[N_EXPERTS - 1]
        n_tiles = xs_ref.shape[0] // EXPERT_TILE
        lax.fori_loop(0, N_EXPERTS, lambda e, c: fill(e, False), 0)
        lax.fori_loop(n_used, n_tiles, lambda i, c: fill_unused(i, False), 0)
        lax.fori_loop(0, N_EXPERTS, lambda e, c: fill(e, True), 0)
        lax.fori_loop(n_used, n_tiles, lambda i, c: fill_unused(i, True), 0)

    for t in range(tm):
        for slot in range(2):
            _row_copy(h_ref, t, xs_ref, pos_ref[slot, t], sem).start(priority=slot)
    for _ in range(2):
        pltpu.make_async_copy(h_ref, xs_ref.at[pl.ds(0, tm)], sem).wait()


def _dispatch(pos, tile_end, h1_flat, n_rows, *, tm):
    tp, d = h1_flat.shape
    return pl.pallas_call(
        functools.partial(_dispatch_kernel, tm=tm),
        grid=(tp // tm,),
        in_specs=[pl.BlockSpec((None, 2, tm), lambda i: (i, 0, 0), memory_space=pltpu.SMEM),
                  pl.BlockSpec(memory_space=pltpu.SMEM),
                  pl.BlockSpec((tm, d), lambda i: (i, 0))],
        out_specs=pl.BlockSpec(memory_space=pl.ANY),
        out_shape=jax.ShapeDtypeStruct((n_rows, d), h1_flat.dtype),
        scratch_shapes=[pltpu.VMEM((EXPERT_TILE, d), h1_flat.dtype), pltpu.SemaphoreType.DMA(())],
        compiler_params=_cparams(("arbitrary",)),
        name="moe_dispatch",
    )(pos, tile_end, h1_flat)


def _expert_kernel(te_ref, nused_ref, xs_ref, w1_ref, w3_ref, w2_ref, ys_ref):
    i = pl.program_id(0)

    @pl.when(i < nused_ref[0])
    def _():
        x = xs_ref[...].astype(BF16)
        a = _dot(x, w1_ref[...].astype(BF16))
        b = _dot(x, w3_ref[...].astype(BF16))
        hmid = (_silu(a) * b).astype(BF16)
        ys_ref[...] = _dot(hmid, w2_ref[...].astype(BF16))

    @pl.when(i >= nused_ref[0])
    def _():
        ys_ref[...] = jnp.zeros_like(ys_ref)


def _experts(tile_expert, n_used, xs, w1, w3, w2, layer):
    rows, dp = xs.shape
    tme = EXPERT_TILE
    d, f = w1.shape[-2:]
    grid_spec = pltpu.PrefetchScalarGridSpec(
        num_scalar_prefetch=2,
        grid=(rows // tme,),
        in_specs=[pl.BlockSpec((tme, dp), lambda i, te, nu: (jnp.minimum(i, nu[0] - 1), 0)),
                  pl.BlockSpec((None, None, d, f), lambda i, te, nu: (layer, te[i], 0, 0)),
                  pl.BlockSpec((None, None, d, f), lambda i, te, nu: (layer, te[i], 0, 0)),
                  pl.BlockSpec((None, None, f, d), lambda i, te, nu: (layer, te[i], 0, 0))],
        out_specs=pl.BlockSpec((tme, dp), lambda i, te, nu: (i, 0)),
    )
    return pl.pallas_call(
        _expert_kernel,
        grid_spec=grid_spec,
        out_shape=jax.ShapeDtypeStruct((rows, dp), F32),
        compiler_params=_cparams(("arbitrary",)),
        name="moe_experts",
    )(tile_expert, n_used, xs, w1, w3, w2)


def _combine_kernel(pos_ref, pos_next_ref, route_ref, h1_ref, g_ref, b_ref, ys_ref,
                    o_ref, y0_ref, y1_ref, sem, *, tm, l, alpha, nsteps):
    step = pl.program_id(0) * pl.num_programs(1) + pl.program_id(1)
    cur = step % 2
    nxt = 1 - cur

    def gather(table, buf):
        for t in range(tm):
            _row_copy(ys_ref, table[0, t], y0_ref.at[buf], t, sem.at[buf]).start(priority=0)
            _row_copy(ys_ref, table[1, t], y1_ref.at[buf], t, sem.at[buf]).start(priority=1)

    def wait(buf):
        pltpu.make_async_copy(ys_ref.at[pl.ds(0, tm)], y0_ref.at[buf], sem.at[buf]).wait()
        pltpu.make_async_copy(ys_ref.at[pl.ds(0, tm)], y1_ref.at[buf], sem.at[buf]).wait()

    @pl.when(step == 0)
    def _():
        gather(pos_ref, 0)

    gather(pos_next_ref, nxt)
    wait(cur)

    lane = lax.broadcasted_iota(jnp.int32, (1, LANES), 1)
    route = route_ref[...]
    gate1 = jnp.sum(jnp.where(lane == 4, route, 0.0), axis=-1, keepdims=True)
    gate2 = jnp.sum(jnp.where(lane == 5, route, 0.0), axis=-1, keepdims=True)
    ffn = gate1 * y0_ref[cur] + gate2 * y1_ref[cur]
    y = _layer_norm(alpha * h1_ref[...] + ffn, g_ref[...], b_ref[...])
    o_ref[...] = jnp.where(_row_valid(pl.program_id(1), tm, l), y, 0.0)

    @pl.when(step == nsteps - 1)
    def _():
        wait(nxt)


def _combine(pos, route, h1, g, b, ys, *, tm, l, alpha):
    bsz, lp, d = h1.shape
    nj = lp // tm
    row = lambda w: pl.BlockSpec((None, tm, w), lambda i, j: (i, j, 0))
    full = lambda s: pl.BlockSpec(s, lambda i, j: (0, 0))
    nsteps = bsz * nj
    return pl.pallas_call(
        functools.partial(_combine_kernel, tm=tm, l=l, alpha=alpha, nsteps=nsteps),
        grid=(bsz, nj),
        in_specs=[pl.BlockSpec((None, 2, tm), lambda i, j: (i * nj + j, 0, 0), memory_space=pltpu.SMEM),
                  pl.BlockSpec((None, 2, tm), lambda i, j: (jnp.minimum(i * nj + j + 1, nsteps - 1), 0, 0),
                               memory_space=pltpu.SMEM),
                  row(LANES), row(d), full((1, d)), full((1, d)), pl.BlockSpec(memory_space=pl.ANY)],
        out_specs=row(d),
        out_shape=jax.ShapeDtypeStruct((bsz, lp, d), F32),
        scratch_shapes=[pltpu.VMEM((2, tm, d), F32), pltpu.VMEM((2, tm, d), F32),
                        pltpu.SemaphoreType.DMA((2,))],
        compiler_params=_cparams(("arbitrary", "arbitrary")),
        name="moe_combine_ln2",
    )(pos, pos, route, h1, g, b, ys)


def kernel(x, meta_tokens, ln_in_g, ln_in_b, w_in, conv_w, a_log, dt_bias, sb_norm_g, gdn_norm_g, w_out,
           ln1_g, ln1_b, w_group, b_group, w_expert, b_expert, w1, w3, w2, ln2_g, ln2_b):
    bsz, seq, d = x.shape
    depth = w_in.shape[0]
    l = seq + N_META
    lp = -(-(FRONT + l) // LANES) * LANES
    tm = 384 if lp % 384 == 0 else LANES
    tp = bsz * lp
    alpha = float((2 * depth) ** 0.25)
    n_tiles = 2 * tp // EXPERT_TILE + N_EXPERTS
    row2 = lambda a: a.reshape(1, -1)

    meta = jnp.broadcast_to(meta_tokens.astype(x.dtype)[None], (bsz, N_META, d))
    xp = jnp.concatenate([jnp.zeros((bsz, FRONT, d), x.dtype), meta, x,
                          jnp.zeros((bsz, lp - FRONT - l, d), x.dtype)], axis=1)
    h = _ln_in(xp, row2(ln_in_g), row2(ln_in_b), tm=tm, l=l)

    n_main = 3 * SB_WIDTH + 4 * GDN_WIDTH
    for i in range(depth):
        w_small = jnp.pad(w_in[i, :, n_main:], ((0, 0), (0, LANES - 2 * GDN_HEADS)))
        q, k, v, gqkv, gz, gates = _inproj(h, w_in, w_small, i, tm=tm)
        o_sb = _sb_attention(q, k, v, row2(jnp.tile(sb_norm_g[i], 2)))
        o_gdn = _gdn(a_log[i], dt_bias[i], gqkv, gz, gates, conv_w[i], row2(gdn_norm_g[i]), tm=tm)

        w_router = jnp.pad(jnp.concatenate([w_group[i], w_expert[i]], axis=1),
                           ((0, 0), (0, LANES - N_GROUPS - N_EXPERTS)))
        b_router = jnp.pad(jnp.concatenate([b_group[i], b_expert[i]]), (0, LANES - N_GROUPS - N_EXPERTS))
        h1, route, counts = _route(o_sb, o_gdn, h, w_out[i], row2(ln1_g[i]), row2(ln1_b[i]),
                                   w_router, row2(b_router), tm=tm, l=l, alpha=alpha)

        ids = route[..., :4].astype(jnp.int32).reshape(tp // tm, tm, 4)
        cnt = counts[0, :N_EXPERTS].astype(jnp.int32)
        tiles = (cnt + EXPERT_TILE - 1) // EXPERT_TILE
        tile_end = jnp.cumsum(tiles)
        base = (tile_end - tiles) * EXPERT_TILE
        tile_expert = jnp.minimum(
            jnp.sum(jnp.arange(n_tiles, dtype=jnp.int32)[:, None] >= tile_end[None, :], axis=1),
            N_EXPERTS - 1).astype(jnp.int32)
        n_used = tile_end[-1:].astype(jnp.int32)
        is_expert = ids[..., 0:2, None] == jnp.arange(N_EXPERTS, dtype=jnp.int32)
        seg_base = jnp.sum(jnp.where(is_expert, base, 0), axis=-1)
        pos = (seg_base + ids[..., 2:4]).transpose(0, 2, 1).astype(jnp.int32)

        xs = _dispatch(pos, tile_end.astype(jnp.int32), h1.reshape(tp, d), n_tiles * EXPERT_TILE, tm=tm)
        ys = _experts(tile_expert, n_used, xs, w1, w3, w2, i)
        h = _combine(pos, route, h1, row2(ln2_g[i]), row2(ln2_b[i]), ys, tm=tm, l=l, alpha=alpha)

    return h[:, FRONT + N_META:FRONT + l]
```

```python
import functools

import jax
import jax.numpy as jnp
from jax import lax
from jax.experimental import pallas as pl
from jax.experimental.pallas import tpu as pltpu

N_META = 16
SB_HEADS = 8
SB_HEAD_DIM = 64
SB_WIDTH = SB_HEADS * SB_HEAD_DIM
GDN_HEADS = 4
GDN_HEAD_DIM = 128
GDN_WIDTH = GDN_HEADS * GDN_HEAD_DIM
GDN_CHUNK = 64
CONV_WIDTH = 4
N_GROUPS = 4
EXPERTS_PER_GROUP = 8
N_EXPERTS = N_GROUPS * EXPERTS_PER_GROUP
D_EXPERT = 256
LN_EPS = 1e-5
RMS_EPS = 1e-6

LANES = 128
SB_BLOCK = 128
FRONT = (-N_META) % GDN_CHUNK
EXPERT_TILE = 512
VMEM_LIMIT = 56 * 1024 * 1024
EXP_UNDERFLOW = -88.0
NEG_BIG = -1e30

F32 = jnp.float32
BF16 = jnp.bfloat16


def _cparams(sem):
    return pltpu.CompilerParams(dimension_semantics=sem, vmem_limit_bytes=VMEM_LIMIT)


def _dot(a, b):
    return jnp.dot(a, b, preferred_element_type=F32)


def _dot_nt(a, b):
    return lax.dot_general(a, b, (((1,), (1,)), ((), ())), preferred_element_type=F32)


def _dot_tn(a, b):
    return lax.dot_general(a, b, (((0,), (0,)), ((), ())), preferred_element_type=F32)


def _split3(x):
    hi = x.astype(BF16)
    r1 = x - hi.astype(F32)
    mid = r1.astype(BF16)
    lo = (r1 - mid.astype(F32)).astype(BF16)
    return hi, mid, lo


def _dot_sel(m01, x):
    hi, mid, lo = _split3(x)
    return _dot(m01, hi) + _dot(m01, mid) + _dot(m01, lo)


def _silu(x):
    return x / (1.0 + jnp.exp(-x))


def _softplus(x):
    return jnp.maximum(x, 0.0) + jnp.log1p(jnp.exp(-jnp.abs(x)))


def _layer_norm(x, g, b):
    mu = jnp.mean(x, axis=-1, keepdims=True)
    xc = x - mu
    var = jnp.mean(xc * xc, axis=-1, keepdims=True)
    return xc * lax.rsqrt(var + LN_EPS) * g + b


def _row_valid(j, tm, l):
    r = j * tm + lax.broadcasted_iota(jnp.int32, (tm, 1), 0)
    return (r >= FRONT) & (r < FRONT + l)


def _ln_in_kernel(x_ref, g_ref, b_ref, o_ref, *, tm, l):
    y = _layer_norm(x_ref[...], g_ref[...], b_ref[...])
    o_ref[...] = jnp.where(_row_valid(pl.program_id(1), tm, l), y, 0.0)


def _ln_in(xp, g, b, *, tm, l):
    bsz, lp, d = xp.shape
    return pl.pallas_call(
        functools.partial(_ln_in_kernel, tm=tm, l=l),
        grid=(bsz, lp // tm),
        in_specs=[pl.BlockSpec((None, tm, d), lambda i, j: (i, j, 0)),
                  pl.BlockSpec((1, d), lambda i, j: (0, 0)),
                  pl.BlockSpec((1, d), lambda i, j: (0, 0))],
        out_specs=pl.BlockSpec((None, tm, d), lambda i, j: (i, j, 0)),
        out_shape=jax.ShapeDtypeStruct((bsz, lp, d), F32),
        compiler_params=_cparams(("arbitrary", "arbitrary")),
        name="ln_in",
    )(xp, g, b)


def _inproj_kernel(x_ref, w_ref, ws_ref, q_ref, k_ref, v_ref, g_ref, z_ref, gt_ref, wb_ref, wsb_ref):
    first = (pl.program_id(0) == 0) & (pl.program_id(1) == 0)

    @pl.when(first)
    def _():
        wb_ref[...] = w_ref[...].astype(BF16)
        wsb_ref[...] = ws_ref[...].astype(BF16)

    x = x_ref[...].astype(BF16)
    s1, s2, s3 = SB_WIDTH, 2 * SB_WIDTH, 3 * SB_WIDTH
    s4 = s3 + 3 * GDN_WIDTH
    s5 = s4 + GDN_WIDTH
    q_ref[...] = _dot(x, wb_ref[:, 0:s1]).astype(BF16)
    k_ref[...] = _dot(x, wb_ref[:, s1:s2]).astype(BF16)
    v_ref[...] = _dot(x, wb_ref[:, s2:s3]).astype(BF16)
    g_ref[...] = _dot(x, wb_ref[:, s3:s4])
    z_ref[...] = _dot(x, wb_ref[:, s4:s5])
    gt_ref[...] = _dot(x, wsb_ref[...])


def _inproj(h, w_in, w_small, layer, *, tm):
    bsz, lp, d = h.shape
    n_main = 3 * SB_WIDTH + 4 * GDN_WIDTH
    row = lambda w: pl.BlockSpec((None, tm, w), lambda i, j: (i, j, 0))
    shp = lambda w, dt: jax.ShapeDtypeStruct((bsz, lp, w), dt)
    return pl.pallas_call(
        _inproj_kernel,
        grid=(bsz, lp // tm),
        in_specs=[row(d),
                  pl.BlockSpec((None, d, n_main), lambda i, j: (layer, 0, 0)),
                  pl.BlockSpec((d, LANES), lambda i, j: (0, 0))],
        out_specs=[row(SB_WIDTH), row(SB_WIDTH), row(SB_WIDTH), row(3 * GDN_WIDTH), row(GDN_WIDTH), row(LANES)],
        out_shape=[shp(SB_WIDTH, BF16), shp(SB_WIDTH, BF16), shp(SB_WIDTH, BF16),
                   shp(3 * GDN_WIDTH, F32), shp(GDN_WIDTH, F32), shp(LANES, F32)],
        scratch_shapes=[pltpu.VMEM((d, n_main), BF16), pltpu.VMEM((d, LANES), BF16)],
        compiler_params=_cparams(("arbitrary", "arbitrary")),
        name="inproj",
    )(h, w_in, w_small)


def _sb_scores(qs, kw, vw, vis, ucat, carry):
    n = len(qs)
    nk = kw[0].shape[0]
    z = [_dot_nt(qs[i], kw[i]) for i in range(n)]
    if vis is not None:
        z = [jnp.where(vis[i], z[i], NEG_BIG) for i in range(n)]
    lnb = [-(jnp.maximum(z[i], 0.0) + jnp.log(1.0 + jnp.exp(-jnp.abs(z[i])))) for i in range(n)]
    hi = [x.astype(BF16) for x in lnb]
    lo = [(lnb[i] - hi[i].astype(F32)).astype(BF16) for i in range(n)]
    t = [_dot(jnp.concatenate([hi[i], lo[i]], axis=0), ucat) for i in range(n)]
    t = [x[:2 * SB_BLOCK] + x[2 * SB_BLOCK:] for x in t]
    p16 = []
    for i in range(n):
        between = t[i][:, :nk] - lnb[i]
        if carry is not None:
            between = between + carry[i]
        logw = z[i] + lnb[i] + between
        p16.append(jnp.exp(logw).astype(BF16))
    pv = [_dot(p16[i], vw[i]) for i in range(n)]
    return [(pv[i], t[i][:, nk:]) for i in range(n)]


def _sb_kernel(q_ref, k_ref, v_ref, g_ref, o_ref, *, qb):
    blk = SB_BLOCK
    win = 2 * blk
    step = pl.program_id(2)
    head0 = lax.broadcasted_iota(jnp.int32, (1, LANES), 1) < SB_HEAD_DIM
    zero = jnp.zeros((), BF16)

    def selector(n):
        r = lax.broadcasted_iota(jnp.int32, (n, n), 0)
        c = lax.broadcasted_iota(jnp.int32, (n, n), 1)
        return jnp.concatenate([(r >= c).astype(BF16), jnp.ones((n, LANES), BF16)], axis=1)

    ucat_win = selector(win)
    ucat_blk = selector(blk)
    row = lax.broadcasted_iota(jnp.int32, (blk, win), 0)
    col = lax.broadcasted_iota(jnp.int32, (blk, win), 1)

    def merge(pv):
        return jnp.where(head0, pv[:blk], pv[blk:])

    ibs, qss, kws, vws, viss = [], [], [], [], []
    for b in range(qb):
        ib = step * qb + b
        s0 = pl.multiple_of(jnp.maximum(ib - 1, 0) * blk, blk)
        q2 = q_ref[b * blk:(b + 1) * blk, :] * jnp.asarray(SB_HEAD_DIM ** -0.5, BF16)
        vis1 = (s0 + col) < (ib * blk + row)
        ibs.append(ib)
        qss.append(jnp.concatenate([jnp.where(head0, q2, zero), jnp.where(head0, zero, q2)], axis=0))
        kws.append(k_ref[pl.ds(s0, win), :])
        vws.append(v_ref[pl.ds(s0, win), :])
        viss.append(jnp.concatenate([vis1, vis1], axis=0))
    first = _sb_scores(qss, kws, vws, viss, ucat_win, None)

    for b, (pv0, tot) in enumerate(first):
        ib, qs, acc = ibs[b], qss[b], merge(pv0)

        def body(carry, qs=qs):
            j, c, a, _ = carry
            start = pl.multiple_of(j * blk, blk)
            (pv, t), = _sb_scores([qs], [k_ref[pl.ds(start, blk), :]], [v_ref[pl.ds(start, blk), :]],
                                  None, ucat_blk, [c])
            c = c + t
            return j - 1, c, a + merge(pv), jnp.max(c)

        def cond(carry):
            j, _, _, cmax = carry
            return (j >= 0) & (cmax >= EXP_UNDERFLOW)

        _, _, acc, _ = lax.while_loop(cond, body, (jnp.maximum(ib - 1, 0) - 1, tot, acc, jnp.max(tot)))

        sq = acc * acc
        ms0 = jnp.sum(jnp.where(head0, sq, 0.0), axis=-1, keepdims=True)
        ms1 = jnp.sum(jnp.where(head0, 0.0, sq), axis=-1, keepdims=True)
        ms = jnp.where(head0, ms0, ms1) * (1.0 / SB_HEAD_DIM)
        o_ref[b * blk:(b + 1) * blk, :] = acc * lax.rsqrt(ms + RMS_EPS) * g_ref[...]


def _sb_attention(q, k, v, g2):
    bsz, lp, _ = q.shape
    blk = SB_BLOCK
    nblk = lp // blk
    qb = next(c for c in (11, 3, 2, 1) if nblk % c == 0)
    npair = SB_WIDTH // LANES
    return pl.pallas_call(
        functools.partial(_sb_kernel, qb=qb),
        grid=(bsz, npair, nblk // qb),
        in_specs=[pl.BlockSpec((None, qb * blk, LANES), lambda b, p, i: (b, i, p)),
                  pl.BlockSpec((None, lp, LANES), lambda b, p, i: (b, 0, p)),
                  pl.BlockSpec((None, lp, LANES), lambda b, p, i: (b, 0, p)),
                  pl.BlockSpec((1, LANES), lambda b, p, i: (0, 0))],
        out_specs=pl.BlockSpec((None, qb * blk, LANES), lambda b, p, i: (b, i, p)),
        out_shape=jax.ShapeDtypeStruct((bsz, lp, SB_WIDTH), F32),
        compiler_params=_cparams(("arbitrary", "arbitrary", "arbitrary")),
        name="sb_attn",
    )(q, k, v, g2)


def _gdn_kernel(alog_ref, dtb_ref, x_ref, z_ref, gt_ref, cw_ref, gn_ref, o_ref,
                s_ref, halo_ref, qe_ref, oi_ref, xm_ref, nn_ref, cd_ref, *, tm):
    c_len = GDN_CHUNK
    hd = GDN_HEAD_DIM
    nch = tm // c_len
    j = pl.program_id(1)

    @pl.when(j == 0)
    def _():
        s_ref[...] = jnp.zeros_like(s_ref)
        halo_ref[...] = jnp.zeros_like(halo_ref)

    li = lax.broadcasted_iota(jnp.int32, (c_len, c_len), 0)
    lj = lax.broadcasted_iota(jnp.int32, (c_len, c_len), 1)
    lower_incl = li >= lj
    lower_strict = li > lj
    m_cum = jnp.concatenate([lower_incl.astype(BF16), jnp.ones((c_len, c_len), BF16)], axis=0)
    eye = (li == lj).astype(F32)
    lane = lax.broadcasted_iota(jnp.int32, (1, LANES), 1)

    gate_lane = (lane >= GDN_HEADS) & (lane < 2 * GDN_HEADS)
    alog_lane = jnp.zeros((1, LANES), F32)
    dtb_lane = jnp.zeros((1, LANES), F32)
    for h in range(GDN_HEADS):
        alog_lane = jnp.where(lane == GDN_HEADS + h, alog_ref[h], alog_lane)
        dtb_lane = jnp.where(lane == GDN_HEADS + h, dtb_ref[h], dtb_lane)
    rate_lane = jnp.exp(alog_lane)

    def column(x, idx):
        return jnp.sum(jnp.where(lane == idx, x, 0.0), axis=-1, keepdims=True)

    def prepare(chunks):
        per_chunk = []
        for c in chunks:
            if isinstance(c, int):
                r0 = c * c_len
                halo = halo_ref[...] if c == 0 else x_ref[r0 - 8:r0, :]
            else:
                r0 = pl.multiple_of(c * c_len, c_len)
                halo = x_ref[pl.ds(pl.multiple_of(r0 - 8, 8), 8), :]
            xe = jnp.concatenate([halo, x_ref[pl.ds(r0, c_len), :]], axis=0)
            conv = xe[8:] * cw_ref[CONV_WIDTH - 1:CONV_WIDTH, :]
            for tap in range(CONV_WIDTH - 1):
                shift = CONV_WIDTH - 1 - tap
                conv = conv + pltpu.roll(xe, shift, axis=0)[8:] * cw_ref[tap:tap + 1, :]
            y = _silu(conv)
            gt = gt_ref[pl.ds(r0, c_len), :]
            rows = j * tm + r0 + lax.broadcasted_iota(jnp.int32, (c_len, 1), 0)
            valid = rows >= FRONT
            beta_all = jnp.where(valid, 1.0 / (1.0 + jnp.exp(-gt)), 0.0)
            g_all = jnp.where(valid & gate_lane, -rate_lane * _softplus(gt + dtb_lane), 0.0)
            per_chunk.append((c, y, beta_all, g_all))

        gcats = [_dot_sel(m_cum, g_all) for (_, _, _, g_all) in per_chunk]
        gc_rows = [gcat[:c_len].T for gcat in gcats]

        probs = []
        for (c, y, beta_all, _), gcat, gc_row in zip(per_chunk, gcats, gc_rows):
            for h in range(GDN_HEADS):
                qr = y[:, h * hd:(h + 1) * hd]
                kr = y[:, GDN_WIDTH + h * hd:GDN_WIDTH + (h + 1) * hd]
                v = y[:, 2 * GDN_WIDTH + h * hd:2 * GDN_WIDTH + (h + 1) * hd]
                q = qr * lax.rsqrt(jnp.sum(qr * qr, axis=-1, keepdims=True) + RMS_EPS) * (hd ** -0.5)
                k = kr * lax.rsqrt(jnp.sum(kr * kr, axis=-1, keepdims=True) + RMS_EPS)
                beta = column(beta_all, h)
                gc = column(gcat[:c_len], GDN_HEADS + h)
                gtot = column(gcat[c_len:], GDN_HEADS + h)
                diff = gc - gc_row[GDN_HEADS + h:GDN_HEADS + h + 1, :]
                decay = jnp.where(lower_incl, jnp.exp(jnp.where(lower_incl, diff, 0.0)), 0.0)
                kb = k * beta
                probs.append(dict(
                    slot=c * GDN_HEADS + h, q=q, k=k, k16=k.astype(BF16), kb=kb, decay=decay,
                    rhs=jnp.concatenate([kb * jnp.exp(gc), v * beta], axis=1).astype(BF16),
                    qd=q * jnp.exp(gc), ke16=(k * jnp.exp(gtot - gc)).astype(BF16),
                    cd=jnp.exp(gtot[0:1, :])))

        for p in probs:
            kq = _dot_nt(jnp.concatenate([p["kb"], p["q"]], axis=0).astype(BF16), p["k16"])
            p["a"] = jnp.where(lower_strict, kq[:c_len] * p["decay"], 0.0)
            p["qk16"] = (kq[c_len:] * p["decay"]).astype(BF16)
            p["pw"] = p["a"]
            p["t"] = eye - p["a"]
        for _ in range(5):
            for p in probs:
                p16 = p["pw"].astype(BF16)
                p["pw"] = _dot(p16, p16)
            for p in probs:
                p["t"] = p["t"] + _dot(p["t"].astype(BF16), p["pw"].astype(BF16))
        for p in probs:
            p["wu"] = _dot(p["t"].astype(BF16), p["rhs"]).astype(BF16)
        for p in probs:
            p["qkwu"] = _dot(p["qk16"], p["wu"])
        for p in probs:
            p["kewu"] = _dot_tn(p["ke16"], p["wu"])
        for p in probs:
            slot = p["slot"]
            qe_ref[slot] = (p["qd"] - p["qkwu"][:, :hd]).astype(BF16)
            oi_ref[slot] = p["qkwu"][:, hd:]
            xm_ref[slot] = (-p["kewu"][:, :hd]).astype(BF16)
            nn_ref[slot] = p["kewu"][:, hd:]
            cd_ref[slot] = jnp.broadcast_to(p["cd"], (8, LANES))

    def scan(c):
        r0 = c * c_len if isinstance(c, int) else pl.multiple_of(c * c_len, c_len)
        heads = range(GDN_HEADS)
        slots = [c * GDN_HEADS + h for h in heads]
        s = [s_ref[h] for h in heads]
        s16 = [x.astype(BF16) for x in s]
        so = [_dot(qe_ref[slots[h]], s16[h]) for h in heads]
        sx = [_dot(xm_ref[slots[h]], s16[h]) for h in heads]
        for h in heads:
            s_ref[h] = s[h] * cd_ref[slots[h]][0:1, :] + sx[h] + nn_ref[slots[h]]
        for h in heads:
            o = so[h] + oi_ref[slots[h]]
            zh = z_ref[pl.ds(r0, c_len), h * hd:(h + 1) * hd]
            on = o * lax.rsqrt(jnp.mean(o * o, axis=-1, keepdims=True) + RMS_EPS) * gn_ref[...]
            o_ref[pl.ds(r0, c_len), h * hd:(h + 1) * hd] = on * _silu(zh)

    grp = next(c for c in (6, 3, 2) if nch % c == 0)
    prepare(list(range(grp)))

    def pipelined(it, _):
        for g in range(grp):
            scan((it - 1) * grp + g)
        prepare([it * grp + g for g in range(grp)])
        return 0

    lax.fori_loop(1, nch // grp, pipelined, 0)
    for g in range(grp):
        scan(nch - grp + g)
    halo_ref[...] = x_ref[tm - 8:tm, :]


def _gdn(a_log, dt_bias, gqkv, gz, gates, conv_w, gn, *, tm):
    bsz, lp, _ = gqkv.shape
    nslot = (tm // GDN_CHUNK) * GDN_HEADS
    row = lambda w: pl.BlockSpec((None, tm, w), lambda i, j: (i, j, 0))
    smem = pl.BlockSpec(memory_space=pltpu.SMEM)
    return pl.pallas_call(
        functools.partial(_gdn_kernel, tm=tm),
        grid=(bsz, lp // tm),
        in_specs=[smem, smem, row(3 * GDN_WIDTH), row(GDN_WIDTH), row(LANES),
                  pl.BlockSpec((CONV_WIDTH, 3 * GDN_WIDTH), lambda i, j: (0, 0)),
                  pl.BlockSpec((1, GDN_HEAD_DIM), lambda i, j: (0, 0))],
        out_specs=row(GDN_WIDTH),
        out_shape=jax.ShapeDtypeStruct((bsz, lp, GDN_WIDTH), F32),
        scratch_shapes=[pltpu.VMEM((GDN_HEADS, GDN_HEAD_DIM, GDN_HEAD_DIM), F32),
                        pltpu.VMEM((8, 3 * GDN_WIDTH), F32),
                        pltpu.VMEM((nslot, GDN_CHUNK, GDN_HEAD_DIM), BF16),
                        pltpu.VMEM((nslot, GDN_CHUNK, GDN_HEAD_DIM), F32),
                        pltpu.VMEM((nslot, GDN_HEAD_DIM, GDN_HEAD_DIM), BF16),
                        pltpu.VMEM((nslot, GDN_HEAD_DIM, GDN_HEAD_DIM), F32),
                        pltpu.VMEM((nslot, 8, LANES), F32)],
        compiler_params=_cparams(("arbitrary", "arbitrary")),
        name="gdn",
    )(a_log, dt_bias, gqkv, gz, gates, conv_w, gn)


def _route_kernel(osb_ref, ogdn_ref, h_ref, wo_ref, g_ref, b_ref, wr_ref, br_ref,
                  h1_ref, route_ref, cnt_ref, wob_ref, wrh_ref, wrl_ref, carry_ref, *, tm, l, alpha):
    first = (pl.program_id(0) == 0) & (pl.program_id(1) == 0)

    @pl.when(first)
    def _():
        wob_ref[...] = wo_ref[...].astype(BF16)
        wr_hi = wr_ref[...].astype(BF16)
        wrh_ref[...] = wr_hi
        wrl_ref[...] = (wr_ref[...] - wr_hi.astype(F32)).astype(BF16)
        carry_ref[...] = jnp.zeros_like(carry_ref)

    mix = (_dot(osb_ref[...].astype(BF16), wob_ref[0:SB_WIDTH, :])
           + _dot(ogdn_ref[...].astype(BF16), wob_ref[SB_WIDTH:, :]))
    y = _layer_norm(alpha * h_ref[...] + mix, g_ref[...], b_ref[...])
    h1 = jnp.where(_row_valid(pl.program_id(1), tm, l), y, 0.0)
    h1_ref[...] = h1

    h_hi = h1.astype(BF16)
    h_lo = (h1 - h_hi.astype(F32)).astype(BF16)
    lg = (_dot(h_hi, wrh_ref[...]) + _dot(h_lo, wrh_ref[...]) + _dot(h_hi, wrl_ref[...])) + br_ref[...]
    lane = lax.broadcasted_iota(jnp.int32, (1, LANES), 1).astype(F32)
    big = jnp.float32(1e9)
    gmask = lane < N_GROUPS
    gl = jnp.where(gmask, lg, NEG_BIG)
    gmax = jnp.max(gl, axis=-1, keepdims=True)
    gidx = jnp.min(jnp.where(gl == gmax, lane, big), axis=-1, keepdims=True)
    g_val = 1.0 / jnp.sum(jnp.where(gmask, jnp.exp(gl - gmax), 0.0), axis=-1, keepdims=True)
    lo = N_GROUPS + EXPERTS_PER_GROUP * gidx
    emask = (lane >= lo) & (lane < lo + EXPERTS_PER_GROUP)
    el = jnp.where(emask, lg, NEG_BIG)
    v1 = jnp.max(el, axis=-1, keepdims=True)
    i1 = jnp.min(jnp.where(el == v1, lane, big), axis=-1, keepdims=True)
    el2 = jnp.where(lane == i1, NEG_BIG, el)
    v2 = jnp.max(el2, axis=-1, keepdims=True)
    i2 = jnp.min(jnp.where(el2 == v2, lane, big), axis=-1, keepdims=True)
    e21 = jnp.exp(v2 - v1)
    w1 = 1.0 / (1.0 + e21)
    w2 = e21 * w1
    e1 = i1 - N_GROUPS
    e2 = i2 - N_GROUPS

    onehot = ((lane == e1) | (lane == e2)).astype(BF16)
    ri = lax.broadcasted_iota(jnp.int32, (tm, tm), 0)
    ci = lax.broadcasted_iota(jnp.int32, (tm, tm), 1)
    before = _dot((ri > ci).astype(BF16), onehot) + carry_ref[...]
    r1 = jnp.sum(jnp.where(lane == e1, before, 0.0), axis=-1, keepdims=True)
    r2 = jnp.sum(jnp.where(lane == e2, before, 0.0), axis=-1, keepdims=True)
    total = carry_ref[...] + jnp.sum(onehot.astype(F32), axis=0, keepdims=True)
    carry_ref[...] = total
    cnt_ref[...] = total

    out = jnp.zeros((tm, LANES), F32)
    for idx, val in enumerate((e1, e2, r1, r2, g_val * w1, g_val * w2)):
        out = jnp.where(lane == idx, val, out)
    route_ref[...] = out


def _route(o_sb, o_gdn, h, w_out, g, b, w_router, b_router, *, tm, l, alpha):
    bsz, lp, d = h.shape
    row = lambda w: pl.BlockSpec((None, tm, w), lambda i, j: (i, j, 0))
    full = lambda s: pl.BlockSpec(s, lambda i, j: (0, 0))
    return pl.pallas_call(
        functools.partial(_route_kernel, tm=tm, l=l, alpha=alpha),
        grid=(bsz, lp // tm),
        in_specs=[row(SB_WIDTH), row(GDN_WIDTH), row(d), full(w_out.shape), full((1, d)), full((1, d)),
                  full((d, LANES)), full((1, LANES))],
        out_specs=[row(d), row(LANES), full((1, LANES))],
        out_shape=[jax.ShapeDtypeStruct((bsz, lp, d), F32),
                   jax.ShapeDtypeStruct((bsz, lp, LANES), F32),
                   jax.ShapeDtypeStruct((1, LANES), F32)],
        scratch_shapes=[pltpu.VMEM(w_out.shape, BF16), pltpu.VMEM((d, LANES), BF16),
                        pltpu.VMEM((d, LANES), BF16), pltpu.VMEM((1, LANES), F32)],
        compiler_params=_cparams(("arbitrary", "arbitrary")),
        name="outproj_ln1_route",
    )(o_sb, o_gdn, h, w_out, g, b, w_router, b_router)


def _row_copy(src, s, dst, d, sem):
    return pltpu.make_async_copy(src.at[pl.ds(s, 1)], dst.at[pl.ds(d, 1)], sem)


def _dispatch_kernel(pos_ref, tile_end_ref, h_ref, xs_ref, zero_ref, sem, *, tm):

    @pl.when(pl.program_id(0) == 0)
    def _():
        zero_ref[...] = jnp.zeros_like(zero_ref)

        def fill(e, wait):
            first_tile = jnp.where(e == 0, 0, tile_end_ref[jnp.maximum(e - 1, 0)])

            @pl.when(tile_end_ref[e] > first_tile)
            def _():
                dst = pl.multiple_of((tile_end_ref[e] - 1) * EXPERT_TILE, EXPERT_TILE)
                cp = pltpu.make_async_copy(zero_ref, xs_ref.at[pl.ds(dst, EXPERT_TILE)], sem)
                if wait:
                    cp.wait()
                else:
                    cp.start()
            return 0

        def fill_unused(i, wait):
            dst = pl.multiple_of(i * EXPERT_TILE, EXPERT_TILE)
            cp = pltpu.make_async_copy(zero_ref, xs_ref.at[pl.ds(dst, EXPERT_TILE)], sem)
            cp.wait() if wait else cp.start()
            return 0

        n_used = tile_end_ref[N_EXPERTS - 1]
        n_tiles = xs_ref.shape[0] // EXPERT_TILE
        lax.fori_loop(0, N_EXPERTS, lambda e, c: fill(e, False), 0)
        lax.fori_loop(n_used, n_tiles, lambda i, c: fill_unused(i, False), 0)
        lax.fori_loop(0, N_EXPERTS, lambda e, c: fill(e, True), 0)
        lax.fori_loop(n_used, n_tiles, lambda i, c: fill_unused(i, True), 0)

    for t in range(tm):
        for slot in range(2):
            _row_copy(h_ref, t, xs_ref, pos_ref[slot, t], sem).start(priority=slot)
    for _ in range(2):
        pltpu.make_async_copy(h_ref, xs_ref.at[pl.ds(0, tm)], sem).wait()


def _dispatch(pos, tile_end, h1_flat, n_rows, *, tm):
    tp, d = h1_flat.shape
    return pl.pallas_call(
        functools.partial(_dispatch_kernel, tm=tm),
        grid=(tp // tm,),
        in_specs=[pl.BlockSpec((None, 2, tm), lambda i: (i, 0, 0), memory_space=pltpu.SMEM),
                  pl.BlockSpec(memory_space=pltpu.SMEM),
                  pl.BlockSpec((tm, d), lambda i: (i, 0))],
        out_specs=pl.BlockSpec(memory_space=pl.ANY),
        out_shape=jax.ShapeDtypeStruct((n_rows, d), h1_flat.dtype),
        scratch_shapes=[pltpu.VMEM((EXPERT_TILE, d), h1_flat.dtype), pltpu.SemaphoreType.DMA(())],
        compiler_params=_cparams(("arbitrary",)),
        name="moe_dispatch",
    )(pos, tile_end, h1_flat)


def _expert_kernel(te_ref, nused_ref, xs_ref, w1_ref, w3_ref, w2_ref, ys_ref):
    i = pl.program_id(0)

    @pl.when(i < nused_ref[0])
    def _():
        x = xs_ref[...].astype(BF16)
        a = _dot(x, w1_ref[...].astype(BF16))
        b = _dot(x, w3_ref[...].astype(BF16))
        hmid = (_silu(a) * b).astype(BF16)
        ys_ref[...] = _dot(hmid, w2_ref[...].astype(BF16))

    @pl.when(i >= nused_ref[0])
    def _():
        ys_ref[...] = jnp.zeros_like(ys_ref)


def _experts(tile_expert, n_used, xs, w1, w3, w2, layer):
    rows, dp = xs.shape
    tme = EXPERT_TILE
    d, f = w1.shape[-2:]
    grid_spec = pltpu.PrefetchScalarGridSpec(
        num_scalar_prefetch=2,
        grid=(rows // tme,),
        in_specs=[pl.BlockSpec((tme, dp), lambda i, te, nu: (jnp.minimum(i, nu[0] - 1), 0)),
                  pl.BlockSpec((None, None, d, f), lambda i, te, nu: (layer, te[i], 0, 0)),
                  pl.BlockSpec((None, None, d, f), lambda i, te, nu: (layer, te[i], 0, 0)),
                  pl.BlockSpec((None, None, f, d), lambda i, te, nu: (layer, te[i], 0, 0))],
        out_specs=pl.BlockSpec((tme, dp), lambda i, te, nu: (i, 0)),
    )
    return pl.pallas_call(
        _expert_kernel,
        grid_spec=grid_spec,
        out_shape=jax.ShapeDtypeStruct((rows, dp), F32),
        compiler_params=_cparams(("arbitrary",)),
        name="moe_experts",
    )(tile_expert, n_used, xs, w1, w3, w2)


def _combine_kernel(pos_ref, pos_next_ref, route_ref, h1_ref, g_ref, b_ref, ys_ref,
                    o_ref, y0_ref, y1_ref, sem, *, tm, l, alpha, nsteps):
    step = pl.program_id(0) * pl.num_programs(1) + pl.program_id(1)
    cur = step % 2
    nxt = 1 - cur

    def gather(table, buf):
        for t in range(tm):
            _row_copy(ys_ref, table[0, t], y0_ref.at[buf], t, sem.at[buf]).start(priority=0)
            _row_copy(ys_ref, table[1, t], y1_ref.at[buf], t, sem.at[buf]).start(priority=1)

    def wait(buf):
        pltpu.make_async_copy(ys_ref.at[pl.ds(0, tm)], y0_ref.at[buf], sem.at[buf]).wait()
        pltpu.make_async_copy(ys_ref.at[pl.ds(0, tm)], y1_ref.at[buf], sem.at[buf]).wait()

    @pl.when(step == 0)
    def _():
        gather(pos_ref, 0)

    gather(pos_next_ref, nxt)
    wait(cur)

    lane = lax.broadcasted_iota(jnp.int32, (1, LANES), 1)
    route = route_ref[...]
    gate1 = jnp.sum(jnp.where(lane == 4, route, 0.0), axis=-1, keepdims=True)
    gate2 = jnp.sum(jnp.where(lane == 5, route, 0.0), axis=-1, keepdims=True)
    ffn = gate1 * y0_ref[cur] + gate2 * y1_ref[cur]
    y = _layer_norm(alpha * h1_ref[...] + ffn, g_ref[...], b_ref[...])
    o_ref[...] = jnp.where(_row_valid(pl.program_id(1), tm, l), y, 0.0)

    @pl.when(step == nsteps - 1)
    def _():
        wait(nxt)


def _combine(pos, route, h1, g, b, ys, *, tm, l, alpha):
    bsz, lp, d = h1.shape
    nj = lp // tm
    row = lambda w: pl.BlockSpec((None, tm, w), lambda i, j: (i, j, 0))
    full = lambda s: pl.BlockSpec(s, lambda i, j: (0, 0))
    nsteps = bsz * nj
    return pl.pallas_call(
        functools.partial(_combine_kernel, tm=tm, l=l, alpha=alpha, nsteps=nsteps),
        grid=(bsz, nj),
        in_specs=[pl.BlockSpec((None, 2, tm), lambda i, j: (i * nj + j, 0, 0), memory_space=pltpu.SMEM),
                  pl.BlockSpec((None, 2, tm), lambda i, j: (jnp.minimum(i * nj + j + 1, nsteps - 1), 0, 0),
                               memory_space=pltpu.SMEM),
                  row(LANES), row(d), full((1, d)), full((1, d)), pl.BlockSpec(memory_space=pl.ANY)],
        out_specs=row(d),
        out_shape=jax.ShapeDtypeStruct((bsz, lp, d), F32),
        scratch_shapes=[pltpu.VMEM((2, tm, d), F32), pltpu.VMEM((2, tm, d), F32),
                        pltpu.SemaphoreType.DMA((2,))],
        compiler_params=_cparams(("arbitrary", "arbitrary")),
        name="moe_combine_ln2",
    )(pos, pos, route, h1, g, b, ys)


def kernel(x, meta_tokens, ln_in_g, ln_in_b, w_in, conv_w, a_log, dt_bias, sb_norm_g, gdn_norm_g, w_out,
           ln1_g, ln1_b, w_group, b_group, w_expert, b_expert, w1, w3, w2, ln2_g, ln2_b):
    bsz, seq, d = x.shape
    depth = w_in.shape[0]
    l = seq + N_META
    lp = -(-(FRONT + l) // LANES) * LANES
    tm = 384 if lp % 384 == 0 else LANES
    tp = bsz * lp
    alpha = float((2 * depth) ** 0.25)
    n_tiles = 2 * tp // EXPERT_TILE + N_EXPERTS
    row2 = lambda a: a.reshape(1, -1)

    meta = jnp.broadcast_to(meta_tokens.astype(x.dtype)[None], (bsz, N_META, d))
    xp = jnp.concatenate([jnp.zeros((bsz, FRONT, d), x.dtype), meta, x,
                          jnp.zeros((bsz, lp - FRONT - l, d), x.dtype)], axis=1)
    h = _ln_in(xp, row2(ln_in_g), row2(ln_in_b), tm=tm, l=l)

    n_main = 3 * SB_WIDTH + 4 * GDN_WIDTH
    for i in range(depth):
        w_small = jnp.pad(w_in[i, :, n_main:], ((0, 0), (0, LANES - 2 * GDN_HEADS)))
        q, k, v, gqkv, gz, gates = _inproj(h, w_in, w_small, i, tm=tm)
        o_sb = _sb_attention(q, k, v, row2(jnp.tile(sb_norm_g[i], 2)))
        o_gdn = _gdn(a_log[i], dt_bias[i], gqkv, gz, gates, conv_w[i], row2(gdn_norm_g[i]), tm=tm)

        w_router = jnp.pad(jnp.concatenate([w_group[i], w_expert[i]], axis=1),
                           ((0, 0), (0, LANES - N_GROUPS - N_EXPERTS)))
        b_router = jnp.pad(jnp.concatenate([b_group[i], b_expert[i]]), (0, LANES - N_GROUPS - N_EXPERTS))
        h1, route, counts = _route(o_sb, o_gdn, h, w_out[i], row2(ln1_g[i]), row2(ln1_b[i]),
                                   w_router, row2(b_router), tm=tm, l=l, alpha=alpha)

        ids = route[..., :4].astype(jnp.int32).reshape(tp // tm, tm, 4)
        cnt = counts[0, :N_EXPERTS].astype(jnp.int32)
        tiles = (cnt + EXPERT_TILE - 1) // EXPERT_TILE
        tile_end = jnp.cumsum(tiles)
        base = (tile_end - tiles) * EXPERT_TILE
        tile_expert = jnp.minimum(
            jnp.sum(jnp.arange(n_tiles, dtype=jnp.int32)[:, None] >= tile_end[None, :], axis=1),
            N_EXPERTS - 1).astype(jnp.int32)
        n_used = tile_end[-1:].astype(jnp.int32)
        is_expert = ids[..., 0:2, None] == jnp.arange(N_EXPERTS, dtype=jnp.int32)
        seg_base = jnp.sum(jnp.where(is_expert, base, 0), axis=-1)
        pos = (seg_base + ids[..., 2:4]).transpose(0, 2, 1).astype(jnp.int32)

        xs = _dispatch(pos, tile_end.astype(jnp.int32), h1.reshape(tp, d), n_tiles * EXPERT_TILE, tm=tm)
        ys = _experts(tile_expert, n_used, xs, w1, w3, w2, i)
        h = _combine(pos, route, h1, row2(ln2_g[i]), row2(ln2_b[i]), ys, tm=tm, l=l, alpha=alpha)

    return h[:, FRONT + N_META:FRONT + l]
```

```python
import functools

import jax
import jax.numpy as jnp
from jax import lax
from jax.experimental import pallas as pl
from jax.experimental.pallas import tpu as pltpu

N_META = 16
SB_HEADS = 8
SB_HEAD_DIM = 64
SB_WIDTH = SB_HEADS * SB_HEAD_DIM
GDN_HEADS = 4
GDN_HEAD_DIM = 128
GDN_WIDTH = GDN_HEADS * GDN_HEAD_DIM
GDN_CHUNK = 64
CONV_WIDTH = 4
N_GROUPS = 4
EXPERTS_PER_GROUP = 8
N_EXPERTS = N_GROUPS * EXPERTS_PER_GROUP
D_EXPERT = 256
LN_EPS = 1e-5
RMS_EPS = 1e-6

LANES = 128
SB_BLOCK = 128
FRONT = (-N_META) % GDN_CHUNK
EXPERT_TILE = 512
VMEM_LIMIT = 56 * 1024 * 1024
EXP_UNDERFLOW = -88.0
NEG_BIG = -1e30

F32 = jnp.float32
BF16 = jnp.bfloat16


def _cparams(sem):
    return pltpu.CompilerParams(dimension_semantics=sem, vmem_limit_bytes=VMEM_LIMIT)


def _dot(a, b):
    return jnp.dot(a, b, preferred_element_type=F32)


def _dot_nt(a, b):
    return lax.dot_general(a, b, (((1,), (1,)), ((), ())), preferred_element_type=F32)


def _dot_tn(a, b):
    return lax.dot_general(a, b, (((0,), (0,)), ((), ())), preferred_element_type=F32)


def _split3(x):
    hi = x.astype(BF16)
    r1 = x - hi.astype(F32)
    mid = r1.astype(BF16)
    lo = (r1 - mid.astype(F32)).astype(BF16)
    return hi, mid, lo


def _dot_sel(m01, x):
    hi, mid, lo = _split3(x)
    return _dot(m01, hi) + _dot(m01, mid) + _dot(m01, lo)


def _silu(x):
    return x / (1.0 + jnp.exp(-x))


def _softplus(x):
    return jnp.maximum(x, 0.0) + jnp.log1p(jnp.exp(-jnp.abs(x)))


def _layer_norm(x, g, b):
    mu = jnp.mean(x, axis=-1, keepdims=True)
    xc = x - mu
    var = jnp.mean(xc * xc, axis=-1, keepdims=True)
    return xc * lax.rsqrt(var + LN_EPS) * g + b


def _row_valid(j, tm, l):
    r = j * tm + lax.broadcasted_iota(jnp.int32, (tm, 1), 0)
    return (r >= FRONT) & (r < FRONT + l)


def _ln_in_kernel(x_ref, meta_ref, g_ref, b_ref, o_ref, *, l):
    j = pl.program_id(1)

    @pl.when(j == 0)
    def _():
        o_ref[0:FRONT, :] = jnp.zeros((FRONT, o_ref.shape[1]), F32)
        o_ref[FRONT:, :] = _layer_norm(meta_ref[...], g_ref[...], b_ref[...])

    @pl.when(j > 0)
    def _():
        y = _layer_norm(x_ref[...], g_ref[...], b_ref[...])
        o_ref[...] = jnp.where(_row_valid(j, GDN_CHUNK, l), y, 0.0)


def _ln_in(x, meta, g, b, *, lp):
    bsz, seq, d = x.shape
    blk = GDN_CHUNK
    nx = -(-seq // blk)
    return pl.pallas_call(
        functools.partial(_ln_in_kernel, l=seq + N_META),
        grid=(bsz, lp // blk),
        in_specs=[pl.BlockSpec((None, blk, d), lambda i, j: (i, jnp.clip(j - 1, 0, nx - 1), 0)),
                  pl.BlockSpec((N_META, d), lambda i, j: (0, 0)),
                  pl.BlockSpec((1, d), lambda i, j: (0, 0)),
                  pl.BlockSpec((1, d), lambda i, j: (0, 0))],
        out_specs=pl.BlockSpec((None, blk, d), lambda i, j: (i, j, 0)),
        out_shape=jax.ShapeDtypeStruct((bsz, lp, d), F32),
        compiler_params=_cparams(("arbitrary", "arbitrary")),
        name="ln_in",
    )(x, meta, g, b)


def _inproj_kernel(x_ref, w_ref, ws_ref, q_ref, k_ref, v_ref, g_ref, z_ref, gt_ref, wb_ref, wsb_ref):
    first = (pl.program_id(0) == 0) & (pl.program_id(1) == 0)

    @pl.when(first)
    def _():
        wb_ref[...] = w_ref[...].astype(BF16)
        wsb_ref[...] = ws_ref[...].astype(BF16)

    x = x_ref[...].astype(BF16)
    s1, s2, s3 = SB_WIDTH, 2 * SB_WIDTH, 3 * SB_WIDTH
    s4 = s3 + 3 * GDN_WIDTH
    s5 = s4 + GDN_WIDTH
    q_ref[...] = _dot(x, wb_ref[:, 0:s1]).astype(BF16)
    k_ref[...] = _dot(x, wb_ref[:, s1:s2]).astype(BF16)
    v_ref[...] = _dot(x, wb_ref[:, s2:s3]).astype(BF16)
    g_ref[...] = _dot(x, wb_ref[:, s3:s4])
    z_ref[...] = _dot(x, wb_ref[:, s4:s5])
    gt_ref[...] = _dot(x, wsb_ref[...])


def _inproj(h, w_in, w_small, layer, *, tm):
    bsz, lp, d = h.shape
    n_main = 3 * SB_WIDTH + 4 * GDN_WIDTH
    row = lambda w: pl.BlockSpec((None, tm, w), lambda i, j: (i, j, 0))
    shp = lambda w, dt: jax.ShapeDtypeStruct((bsz, lp, w), dt)
    return pl.pallas_call(
        _inproj_kernel,
        grid=(bsz, lp // tm),
        in_specs=[row(d),
                  pl.BlockSpec((None, d, n_main), lambda i, j: (layer, 0, 0)),
                  pl.BlockSpec((d, LANES), lambda i, j: (0, 0))],
        out_specs=[row(SB_WIDTH), row(SB_WIDTH), row(SB_WIDTH), row(3 * GDN_WIDTH), row(GDN_WIDTH), row(LANES)],
        out_shape=[shp(SB_WIDTH, BF16), shp(SB_WIDTH, BF16), shp(SB_WIDTH, BF16),
                   shp(3 * GDN_WIDTH, F32), shp(GDN_WIDTH, F32), shp(LANES, F32)],
        scratch_shapes=[pltpu.VMEM((d, n_main), BF16), pltpu.VMEM((d, LANES), BF16)],
        compiler_params=_cparams(("arbitrary", "arbitrary")),
        name="inproj",
    )(h, w_in, w_small)


def _sb_scores(qs, kw, vw, vis, ucat, carry):
    n = len(qs)
    nk = kw[0].shape[0]
    z = [_dot_nt(qs[i], kw[i]) for i in range(n)]
    if vis is not None:
        z = [jnp.where(vis[i], z[i], NEG_BIG) for i in range(n)]
    lnb = [-(jnp.maximum(z[i], 0.0) + jnp.log(1.0 + jnp.exp(-jnp.abs(z[i])))) for i in range(n)]
    hi = [x.astype(BF16) for x in lnb]
    lo = [(lnb[i] - hi[i].astype(F32)).astype(BF16) for i in range(n)]
    t = [_dot(jnp.concatenate([hi[i], lo[i]], axis=0), ucat) for i in range(n)]
    t = [x[:2 * SB_BLOCK] + x[2 * SB_BLOCK:] for x in t]
    p16 = []
    for i in range(n):
        between = t[i][:, :nk] - lnb[i]
        if carry is not None:
            between = between + carry[i]
        logw = z[i] + lnb[i] + between
        p16.append(jnp.exp(logw).astype(BF16))
    pv = [_dot(p16[i], vw[i]) for i in range(n)]
    return [(pv[i], t[i][:, nk:]) for i in range(n)]


def _sb_kernel(q_ref, k_ref, v_ref, g_ref, o_ref, *, qb):
    blk = SB_BLOCK
    win = 2 * blk
    step = pl.program_id(2)
    head0 = lax.broadcasted_iota(jnp.int32, (1, LANES), 1) < SB_HEAD_DIM
    zero = jnp.zeros((), BF16)

    def selector(n):
        r = lax.broadcasted_iota(jnp.int32, (n, n), 0)
        c = lax.broadcasted_iota(jnp.int32, (n, n), 1)
        return jnp.concatenate([(r >= c).astype(BF16), jnp.ones((n, LANES), BF16)], axis=1)

    ucat_win = selector(win)
    ucat_blk = selector(blk)
    row = lax.broadcasted_iota(jnp.int32, (blk, win), 0)
    col = lax.broadcasted_iota(jnp.int32, (blk, win), 1)

    def merge(pv):
        return jnp.where(head0, pv[:blk], pv[blk:])

    ibs, qss, kws, vws, viss = [], [], [], [], []
    for b in range(qb):
        ib = step * qb + b
        s0 = pl.multiple_of(jnp.maximum(ib - 1, 0) * blk, blk)
        q2 = q_ref[b * blk:(b + 1) * blk, :] * jnp.asarray(SB_HEAD_DIM ** -0.5, BF16)
        vis1 = (s0 + col) < (ib * blk + row)
        ibs.append(ib)
        qss.append(jnp.concatenate([jnp.where(head0, q2, zero), jnp.where(head0, zero, q2)], axis=0))
        kws.append(k_ref[pl.ds(s0, win), :])
        vws.append(v_ref[pl.ds(s0, win), :])
        viss.append(jnp.concatenate([vis1, vis1], axis=0))
    first = _sb_scores(qss, kws, vws, viss, ucat_win, None)

    for b, (pv0, tot) in enumerate(first):
        ib, qs, acc = ibs[b], qss[b], merge(pv0)

        def body(carry, qs=qs):
            j, c, a, _ = carry
            start = pl.multiple_of(j * blk, blk)
            (pv, t), = _sb_scores([qs], [k_ref[pl.ds(start, blk), :]], [v_ref[pl.ds(start, blk), :]],
                                  None, ucat_blk, [c])
            c = c + t
            return j - 1, c, a + merge(pv), jnp.max(c)

        def cond(carry):
            j, _, _, cmax = carry
            return (j >= 0) & (cmax >= EXP_UNDERFLOW)

        _, _, acc, _ = lax.while_loop(cond, body, (jnp.maximum(ib - 1, 0) - 1, tot, acc, jnp.max(tot)))

        sq = acc * acc
        ms0 = jnp.sum(jnp.where(head0, sq, 0.0), axis=-1, keepdims=True)
        ms1 = jnp.sum(jnp.where(head0, 0.0, sq), axis=-1, keepdims=True)
        ms = jnp.where(head0, ms0, ms1) * (1.0 / SB_HEAD_DIM)
        o_ref[b * blk:(b + 1) * blk, :] = acc * lax.rsqrt(ms + RMS_EPS) * g_ref[...]


def _sb_attention(q, k, v, g2):
    bsz, lp, _ = q.shape
    blk = SB_BLOCK
    nblk = lp // blk
    qb = next(c for c in (11, 3, 2, 1) if nblk % c == 0)
    npair = SB_WIDTH // LANES
    return pl.pallas_call(
        functools.partial(_sb_kernel, qb=qb),
        grid=(bsz, npair, nblk // qb),
        in_specs=[pl.BlockSpec((None, qb * blk, LANES), lambda b, p, i: (b, i, p)),
                  pl.BlockSpec((None, lp, LANES), lambda b, p, i: (b, 0, p)),
                  pl.BlockSpec((None, lp, LANES), lambda b, p, i: (b, 0, p)),
                  pl.BlockSpec((1, LANES), lambda b, p, i: (0, 0))],
        out_specs=pl.BlockSpec((None, qb * blk, LANES), lambda b, p, i: (b, i, p)),
        out_shape=jax.ShapeDtypeStruct((bsz, lp, SB_WIDTH), F32),
        compiler_params=_cparams(("arbitrary", "arbitrary", "arbitrary")),
        name="sb_attn",
    )(q, k, v, g2)


def _gdn_kernel(alog_ref, dtb_ref, x_ref, z_ref, gt_ref, cw_ref, gn_ref, o_ref,
                s_ref, halo_ref, qe_ref, oi_ref, xm_ref, nn_ref, cd_ref, *, tm):
    c_len = GDN_CHUNK
    hd = GDN_HEAD_DIM
    nch = tm // c_len
    j = pl.program_id(1)

    @pl.when(j == 0)
    def _():
        s_ref[...] = jnp.zeros_like(s_ref)
        halo_ref[...] = jnp.zeros_like(halo_ref)

    li = lax.broadcasted_iota(jnp.int32, (c_len, c_len), 0)
    lj = lax.broadcasted_iota(jnp.int32, (c_len, c_len), 1)
    lower_incl = li >= lj
    lower_strict = li > lj
    m_cum = jnp.concatenate([lower_incl.astype(BF16), jnp.ones((c_len, c_len), BF16)], axis=0)
    eye = (li == lj).astype(F32)
    lane = lax.broadcasted_iota(jnp.int32, (1, LANES), 1)

    gate_lane = (lane >= GDN_HEADS) & (lane < 2 * GDN_HEADS)
    alog_lane = jnp.zeros((1, LANES), F32)
    dtb_lane = jnp.zeros((1, LANES), F32)
    for h in range(GDN_HEADS):
        alog_lane = jnp.where(lane == GDN_HEADS + h, alog_ref[h], alog_lane)
        dtb_lane = jnp.where(lane == GDN_HEADS + h, dtb_ref[h], dtb_lane)
    rate_lane = jnp.exp(alog_lane)

    def column(x, idx):
        return jnp.sum(jnp.where(lane == idx, x, 0.0), axis=-1, keepdims=True)

    def prepare(chunks):
        per_chunk = []
        for c in chunks:
            if isinstance(c, int):
                r0 = c * c_len
                halo = halo_ref[...] if c == 0 else x_ref[r0 - 8:r0, :]
            else:
                r0 = pl.multiple_of(c * c_len, c_len)
                halo = x_ref[pl.ds(pl.multiple_of(r0 - 8, 8), 8), :]
            xe = jnp.concatenate([halo, x_ref[pl.ds(r0, c_len), :]], axis=0)
            conv = xe[8:] * cw_ref[CONV_WIDTH - 1:CONV_WIDTH, :]
            for tap in range(CONV_WIDTH - 1):
                shift = CONV_WIDTH - 1 - tap
                conv = conv + pltpu.roll(xe, shift, axis=0)[8:] * cw_ref[tap:tap + 1, :]
            y = _silu(conv)
            gt = gt_ref[pl.ds(r0, c_len), :]
            rows = j * tm + r0 + lax.broadcasted_iota(jnp.int32, (c_len, 1), 0)
            valid = rows >= FRONT
            beta_all = jnp.where(valid, 1.0 / (1.0 + jnp.exp(-gt)), 0.0)
            g_all = jnp.where(valid & gate_lane, -rate_lane * _softplus(gt + dtb_lane), 0.0)
            per_chunk.append((c, y, beta_all, g_all))

        gcats = [_dot_sel(m_cum, g_all) for (_, _, _, g_all) in per_chunk]
        gc_rows = [gcat[:c_len].T for gcat in gcats]

        probs = []
        for (c, y, beta_all, _), gcat, gc_row in zip(per_chunk, gcats, gc_rows):
            for h in range(GDN_HEADS):
                qr = y[:, h * hd:(h + 1) * hd]
                kr = y[:, GDN_WIDTH + h * hd:GDN_WIDTH + (h + 1) * hd]
                v = y[:, 2 * GDN_WIDTH + h * hd:2 * GDN_WIDTH + (h + 1) * hd]
                q = qr * lax.rsqrt(jnp.sum(qr * qr, axis=-1, keepdims=True) + RMS_EPS) * (hd ** -0.5)
                k = kr * lax.rsqrt(jnp.sum(kr * kr, axis=-1, keepdims=True) + RMS_EPS)
                beta = column(beta_all, h)
                gc = column(gcat[:c_len], GDN_HEADS + h)
                gtot = column(gcat[c_len:], GDN_HEADS + h)
                diff = gc - gc_row[GDN_HEADS + h:GDN_HEADS + h + 1, :]
                decay = jnp.where(lower_incl, jnp.exp(jnp.where(lower_incl, diff, 0.0)), 0.0)
                kb = k * beta
                probs.append(dict(
                    slot=c * GDN_HEADS + h, q=q, k=k, k16=k.astype(BF16), kb=kb, decay=decay,
                    rhs=jnp.concatenate([kb * jnp.exp(gc), v * beta], axis=1).astype(BF16),
                    qd=q * jnp.exp(gc), ke16=(k * jnp.exp(gtot - gc)).astype(BF16),
                    cd=jnp.exp(gtot[0:1, :])))

        for p in probs:
            kq = _dot_nt(jnp.concatenate([p["kb"], p["q"]], axis=0).astype(BF16), p["k16"])
            p["a"] = jnp.where(lower_strict, kq[:c_len] * p["decay"], 0.0)
            p["qk16"] = (kq[c_len:] * p["decay"]).astype(BF16)
            p["pw"] = p["a"]
            p["t"] = eye - p["a"]
        for _ in range(5):
            for p in probs:
                p16 = p["pw"].astype(BF16)
                p["pw"] = _dot(p16, p16)
            for p in probs:
                p["t"] = p["t"] + _dot(p["t"].astype(BF16), p["pw"].astype(BF16))
        for p in probs:
            p["wu"] = _dot(p["t"].astype(BF16), p["rhs"]).astype(BF16)
        for p in probs:
            p["qkwu"] = _dot(p["qk16"], p["wu"])
        for p in probs:
            p["kewu"] = _dot_tn(p["ke16"], p["wu"])
        for p in probs:
            slot = p["slot"]
            qe_ref[slot] = (p["qd"] - p["qkwu"][:, :hd]).astype(BF16)
            oi_ref[slot] = p["qkwu"][:, hd:]
            xm_ref[slot] = (-p["kewu"][:, :hd]).astype(BF16)
            nn_ref[slot] = p["kewu"][:, hd:]
            cd_ref[slot] = jnp.broadcast_to(p["cd"], (8, LANES))

    def scan(c):
        r0 = c * c_len if isinstance(c, int) else pl.multiple_of(c * c_len, c_len)
        heads = range(GDN_HEADS)
        slots = [c * GDN_HEADS + h for h in heads]
        s = [s_ref[h] for h in heads]
        s16 = [x.astype(BF16) for x in s]
        so = [_dot(qe_ref[slots[h]], s16[h]) for h in heads]
        sx = [_dot(xm_ref[slots[h]], s16[h]) for h in heads]
        for h in heads:
            s_ref[h] = s[h] * cd_ref[slots[h]][0:1, :] + sx[h] + nn_ref[slots[h]]
        for h in heads:
            o = so[h] + oi_ref[slots[h]]
            zh = z_ref[pl.ds(r0, c_len), h * hd:(h + 1) * hd]
            on = o * lax.rsqrt(jnp.mean(o * o, axis=-1, keepdims=True) + RMS_EPS) * gn_ref[...]
            o_ref[pl.ds(r0, c_len), h * hd:(h + 1) * hd] = on * _silu(zh)

    grp = next(c for c in (6, 3, 2) if nch % c == 0)
    prepare(list(range(grp)))

    def pipelined(it, _):
        for g in range(grp):
            scan((it - 1) * grp + g)
        prepare([it * grp + g for g in range(grp)])
        return 0

    lax.fori_loop(1, nch // grp, pipelined, 0)
    for g in range(grp):
        scan(nch - grp + g)
    halo_ref[...] = x_ref[tm - 8:tm, :]


def _gdn(a_log, dt_bias, gqkv, gz, gates, conv_w, gn, *, tm):
    bsz, lp, _ = gqkv.shape
    nslot = (tm // GDN_CHUNK) * GDN_HEADS
    row = lambda w: pl.BlockSpec((None, tm, w), lambda i, j: (i, j, 0))
    smem = pl.BlockSpec(memory_space=pltpu.SMEM)
    return pl.pallas_call(
        functools.partial(_gdn_kernel, tm=tm),
        grid=(bsz, lp // tm),
        in_specs=[smem, smem, row(3 * GDN_WIDTH), row(GDN_WIDTH), row(LANES),
                  pl.BlockSpec((CONV_WIDTH, 3 * GDN_WIDTH), lambda i, j: (0, 0)),
                  pl.BlockSpec((1, GDN_HEAD_DIM), lambda i, j: (0, 0))],
        out_specs=row(GDN_WIDTH),
        out_shape=jax.ShapeDtypeStruct((bsz, lp, GDN_WIDTH), F32),
        scratch_shapes=[pltpu.VMEM((GDN_HEADS, GDN_HEAD_DIM, GDN_HEAD_DIM), F32),
                        pltpu.VMEM((8, 3 * GDN_WIDTH), F32),
                        pltpu.VMEM((nslot, GDN_CHUNK, GDN_HEAD_DIM), BF16),
                        pltpu.VMEM((nslot, GDN_CHUNK, GDN_HEAD_DIM), F32),
                        pltpu.VMEM((nslot, GDN_HEAD_DIM, GDN_HEAD_DIM), BF16),
                        pltpu.VMEM((nslot, GDN_HEAD_DIM, GDN_HEAD_DIM), F32),
                        pltpu.VMEM((nslot, 8, LANES), F32)],
        compiler_params=_cparams(("arbitrary", "arbitrary")),
        name="gdn",
    )(a_log, dt_bias, gqkv, gz, gates, conv_w, gn)


def _route_kernel(osb_ref, ogdn_ref, h_ref, wo_ref, g_ref, b_ref, wr_ref, br_ref,
                  h1_ref, route_ref, cnt_ref, wob_ref, wrh_ref, wrl_ref, carry_ref, *, tm, l, alpha):
    first = (pl.program_id(0) == 0) & (pl.program_id(1) == 0)

    @pl.when(first)
    def _():
        wob_ref[...] = wo_ref[...].astype(BF16)
        wr_hi = wr_ref[...].astype(BF16)
        wrh_ref[...] = wr_hi
        wrl_ref[...] = (wr_ref[...] - wr_hi.astype(F32)).astype(BF16)
        carry_ref[...] = jnp.zeros_like(carry_ref)

    mix = (_dot(osb_ref[...].astype(BF16), wob_ref[0:SB_WIDTH, :])
           + _dot(ogdn_ref[...].astype(BF16), wob_ref[SB_WIDTH:, :]))
    y = _layer_norm(alpha * h_ref[...] + mix, g_ref[...], b_ref[...])
    h1 = jnp.where(_row_valid(pl.program_id(1), tm, l), y, 0.0)
    h1_ref[...] = h1

    h_hi = h1.astype(BF16)
    h_lo = (h1 - h_hi.astype(F32)).astype(BF16)
    lg = (_dot(h_hi, wrh_ref[...]) + _dot(h_lo, wrh_ref[...]) + _dot(h_hi, wrl_ref[...])) + br_ref[...]
    lane = lax.broadcasted_iota(jnp.int32, (1, LANES), 1).astype(F32)
    big = jnp.float32(1e9)
    gmask = lane < N_GROUPS
    gl = jnp.where(gmask, lg, NEG_BIG)
    gmax = jnp.max(gl, axis=-1, keepdims=True)
    gidx = jnp.min(jnp.where(gl == gmax, lane, big), axis=-1, keepdims=True)
    g_val = 1.0 / jnp.sum(jnp.where(gmask, jnp.exp(gl - gmax), 0.0), axis=-1, keepdims=True)
    lo = N_GROUPS + EXPERTS_PER_GROUP * gidx
    emask = (lane >= lo) & (lane < lo + EXPERTS_PER_GROUP)
    el = jnp.where(emask, lg, NEG_BIG)
    v1 = jnp.max(el, axis=-1, keepdims=True)
    i1 = jnp.min(jnp.where(el == v1, lane, big), axis=-1, keepdims=True)
    el2 = jnp.where(lane == i1, NEG_BIG, el)
    v2 = jnp.max(el2, axis=-1, keepdims=True)
    i2 = jnp.min(jnp.where(el2 == v2, lane, big), axis=-1, keepdims=True)
    e21 = jnp.exp(v2 - v1)
    w1 = 1.0 / (1.0 + e21)
    w2 = e21 * w1
    e1 = i1 - N_GROUPS
    e2 = i2 - N_GROUPS

    onehot = ((lane == e1) | (lane == e2)).astype(BF16)
    ri = lax.broadcasted_iota(jnp.int32, (tm, tm), 0)
    ci = lax.broadcasted_iota(jnp.int32, (tm, tm), 1)
    before = _dot((ri > ci).astype(BF16), onehot) + carry_ref[...]
    r1 = jnp.sum(jnp.where(lane == e1, before, 0.0), axis=-1, keepdims=True)
    r2 = jnp.sum(jnp.where(lane == e2, before, 0.0), axis=-1, keepdims=True)
    total = carry_ref[...] + jnp.sum(onehot.astype(F32), axis=0, keepdims=True)
    carry_ref[...] = total
    cnt_ref[...] = total

    out = jnp.zeros((tm, LANES), F32)
    for idx, val in enumerate((e1, e2, r1, r2, g_val * w1, g_val * w2)):
        out = jnp.where(lane == idx, val, out)
    route_ref[...] = out


def _route(o_sb, o_gdn, h, w_out, g, b, w_router, b_router, *, tm, l, alpha):
    bsz, lp, d = h.shape
    row = lambda w: pl.BlockSpec((None, tm, w), lambda i, j: (i, j, 0))
    full = lambda s: pl.BlockSpec(s, lambda i, j: (0, 0))
    return pl.pallas_call(
        functools.partial(_route_kernel, tm=tm, l=l, alpha=alpha),
        grid=(bsz, lp // tm),
        in_specs=[row(SB_WIDTH), row(GDN_WIDTH), row(d), full(w_out.shape), full((1, d)), full((1, d)),
                  full((d, LANES)), full((1, LANES))],
        out_specs=[row(d), row(LANES), full((1, LANES))],
        out_shape=[jax.ShapeDtypeStruct((bsz, lp, d), F32),
                   jax.ShapeDtypeStruct((bsz, lp, LANES), F32),
                   jax.ShapeDtypeStruct((1, LANES), F32)],
        scratch_shapes=[pltpu.VMEM(w_out.shape, BF16), pltpu.VMEM((d, LANES), BF16),
                        pltpu.VMEM((d, LANES), BF16), pltpu.VMEM((1, LANES), F32)],
        compiler_params=_cparams(("arbitrary", "arbitrary")),
        name="outproj_ln1_route",
    )(o_sb, o_gdn, h, w_out, g, b, w_router, b_router)


def _row_copy(src, s, dst, d, sem):
    return pltpu.make_async_copy(src.at[pl.ds(s, 1)], dst.at[pl.ds(d, 1)], sem)


def _dispatch_kernel(pos_ref, tile_end_ref, h_ref, xs_ref, zero_ref, sem, *, tm):

    @pl.when(pl.program_id(0) == 0)
    def _():
        zero_ref[...] = jnp.zeros_like(zero_ref)

        def fill(e, wait):
            first_tile = jnp.where(e == 0, 0, tile_end_ref[jnp.maximum(e - 1, 0)])

            @pl.when(tile_end_ref[e] > first_tile)
            def _():
                dst = pl.multiple_of((tile_end_ref[e] - 1) * EXPERT_TILE, EXPERT_TILE)
                cp = pltpu.make_async_copy(zero_ref, xs_ref.at[pl.ds(dst, EXPERT_TILE)], sem)
                if wait:
                    cp.wait()
                else:
                    cp.start()
            return 0

        def fill_unused(i, wait):
            dst = pl.multiple_of(i * EXPERT_TILE, EXPERT_TILE)
            cp = pltpu.make_async_copy(zero_ref, xs_ref.at[pl.ds(dst, EXPERT_TILE)], sem)
            cp.wait() if wait else cp.start()
            return 0

        n_used = tile_end_ref[N_EXPERTS - 1]
        n_tiles = xs_ref.shape[0] // EXPERT_TILE
        lax.fori_loop(0, N_EXPERTS, lambda e, c: fill(e, False), 0)
        lax.fori_loop(n_used, n_tiles, lambda i, c: fill_unused(i, False), 0)
        lax.fori_loop(0, N_EXPERTS, lambda e, c: fill(e, True), 0)
        lax.fori_loop(n_used, n_tiles, lambda i, c: fill_unused(i, True), 0)

    for t in range(tm):
        for slot in range(2):
            _row_copy(h_ref, t, xs_ref, pos_ref[slot, t], sem).start(priority=slot)
    for _ in range(2):
        pltpu.make_async_copy(h_ref, xs_ref.at[pl.ds(0, tm)], sem).wait()


def _dispatch(pos, tile_end, h1_flat, n_rows, *, tm):
    tp, d = h1_flat.shape
    return pl.pallas_call(
        functools.partial(_dispatch_kernel, tm=tm),
        grid=(tp // tm,),
        in_specs=[pl.BlockSpec((None, 2, tm), lambda i: (i, 0, 0), memory_space=pltpu.SMEM),
                  pl.BlockSpec(memory_space=pltpu.SMEM),
                  pl.BlockSpec((tm, d), lambda i: (i, 0))],
        out_specs=pl.BlockSpec(memory_space=pl.ANY),
        out_shape=jax.ShapeDtypeStruct((n_rows, d), h1_flat.dtype),
        scratch_shapes=[pltpu.VMEM((EXPERT_TILE, d), h1_flat.dtype), pltpu.SemaphoreType.DMA(())],
        compiler_params=_cparams(("arbitrary",)),
        name="moe_dispatch",
    )(pos, tile_end, h1_flat)


def _expert_kernel(te_ref, nused_ref, xs_ref, w1_ref, w3_ref, w2_ref, ys_ref):
    i = pl.program_id(0)

    @pl.when(i < nused_ref[0])
    def _():
        x = xs_ref[...].astype(BF16)
        a = _dot(x, w1_ref[...].astype(BF16))
        b = _dot(x, w3_ref[...].astype(BF16))
        hmid = (_silu(a) * b).astype(BF16)
        ys_ref[...] = _dot(hmid, w2_ref[...].astype(BF16))

    @pl.when(i >= nused_ref[0])
    def _():
        ys_ref[...] = jnp.zeros_like(ys_ref)


def _experts(tile_expert, n_used, xs, w1, w3, w2, layer):
    rows, dp = xs.shape
    tme = EXPERT_TILE
    d, f = w1.shape[-2:]
    grid_spec = pltpu.PrefetchScalarGridSpec(
        num_scalar_prefetch=2,
        grid=(rows // tme,),
        in_specs=[pl.BlockSpec((tme, dp), lambda i, te, nu: (jnp.minimum(i, nu[0] - 1), 0)),
                  pl.BlockSpec((None, None, d, f), lambda i, te, nu: (layer, te[i], 0, 0)),
                  pl.BlockSpec((None, None, d, f), lambda i, te, nu: (layer, te[i], 0, 0)),
                  pl.BlockSpec((None, None, f, d), lambda i, te, nu: (layer, te[i], 0, 0))],
        out_specs=pl.BlockSpec((tme, dp), lambda i, te, nu: (i, 0)),
    )
    return pl.pallas_call(
        _expert_kernel,
        grid_spec=grid_spec,
        out_shape=jax.ShapeDtypeStruct((rows, dp), F32),
        compiler_params=_cparams(("arbitrary",)),
        name="moe_experts",
    )(tile_expert, n_used, xs, w1, w3, w2)


def _combine_kernel(pos_ref, pos_next_ref, route_ref, h1_ref, g_ref, b_ref, ys_ref,
                    o_ref, y0_ref, y1_ref, sem, *, tm, l, alpha, nsteps):
    step = pl.program_id(0) * pl.num_programs(1) + pl.program_id(1)
    cur = step % 2
    nxt = 1 - cur

    def gather(table, buf):
        for t in range(tm):
            _row_copy(ys_ref, table[0, t], y0_ref.at[buf], t, sem.at[buf]).start(priority=0)
            _row_copy(ys_ref, table[1, t], y1_ref.at[buf], t, sem.at[buf]).start(priority=1)

    def wait(buf):
        pltpu.make_async_copy(ys_ref.at[pl.ds(0, tm)], y0_ref.at[buf], sem.at[buf]).wait()
        pltpu.make_async_copy(ys_ref.at[pl.ds(0, tm)], y1_ref.at[buf], sem.at[buf]).wait()

    @pl.when(step == 0)
    def _():
        gather(pos_ref, 0)

    gather(pos_next_ref, nxt)
    wait(cur)

    lane = lax.broadcasted_iota(jnp.int32, (1, LANES), 1)
    route = route_ref[...]
    gate1 = jnp.sum(jnp.where(lane == 4, route, 0.0), axis=-1, keepdims=True)
    gate2 = jnp.sum(jnp.where(lane == 5, route, 0.0), axis=-1, keepdims=True)
    ffn = gate1 * y0_ref[cur] + gate2 * y1_ref[cur]
    y = _layer_norm(alpha * h1_ref[...] + ffn, g_ref[...], b_ref[...])
    o_ref[...] = jnp.where(_row_valid(pl.program_id(1), tm, l), y, 0.0)

    @pl.when(step == nsteps - 1)
    def _():
        wait(nxt)


def _combine(pos, route, h1, g, b, ys, *, tm, l, alpha):
    bsz, lp, d = h1.shape
    nj = lp // tm
    row = lambda w: pl.BlockSpec((None, tm, w), lambda i, j: (i, j, 0))
    full = lambda s: pl.BlockSpec(s, lambda i, j: (0, 0))
    nsteps = bsz * nj
    return pl.pallas_call(
        functools.partial(_combine_kernel, tm=tm, l=l, alpha=alpha, nsteps=nsteps),
        grid=(bsz, nj),
        in_specs=[pl.BlockSpec((None, 2, tm), lambda i, j: (i * nj + j, 0, 0), memory_space=pltpu.SMEM),
                  pl.BlockSpec((None, 2, tm), lambda i, j: (jnp.minimum(i * nj + j + 1, nsteps - 1), 0, 0),
                               memory_space=pltpu.SMEM),
                  row(LANES), row(d), full((1, d)), full((1, d)), pl.BlockSpec(memory_space=pl.ANY)],
        out_specs=row(d),
        out_shape=jax.ShapeDtypeStruct((bsz, lp, d), F32),
        scratch_shapes=[pltpu.VMEM((2, tm, d), F32), pltpu.VMEM((2, tm, d), F32),
                        pltpu.SemaphoreType.DMA((2,))],
        compiler_params=_cparams(("arbitrary", "arbitrary")),
        name="moe_combine_ln2",
    )(pos, pos, route, h1, g, b, ys)


def kernel(x, meta_tokens, ln_in_g, ln_in_b, w_in, conv_w, a_log, dt_bias, sb_norm_g, gdn_norm_g, w_out,
           ln1_g, ln1_b, w_group, b_group, w_expert, b_expert, w1, w3, w2, ln2_g, ln2_b):
    bsz, seq, d = x.shape
    depth = w_in.shape[0]
    l = seq + N_META
    lp = -(-(FRONT + l) // LANES) * LANES
    tm = 384 if lp % 384 == 0 else LANES
    tp = bsz * lp
    alpha = float((2 * depth) ** 0.25)
    n_tiles = 2 * tp // EXPERT_TILE + N_EXPERTS
    row2 = lambda a: a.reshape(1, -1)

    h = _ln_in(x, meta_tokens.astype(x.dtype), row2(ln_in_g), row2(ln_in_b), lp=lp)

    n_main = 3 * SB_WIDTH + 4 * GDN_WIDTH
    for i in range(depth):
        w_small = jnp.pad(w_in[i, :, n_main:], ((0, 0), (0, LANES - 2 * GDN_HEADS)))
        q, k, v, gqkv, gz, gates = _inproj(h, w_in, w_small, i, tm=tm)
        o_sb = _sb_attention(q, k, v, row2(jnp.tile(sb_norm_g[i], 2)))
        o_gdn = _gdn(a_log[i], dt_bias[i], gqkv, gz, gates, conv_w[i], row2(gdn_norm_g[i]), tm=tm)

        w_router = jnp.pad(jnp.concatenate([w_group[i], w_expert[i]], axis=1),
                           ((0, 0), (0, LANES - N_GROUPS - N_EXPERTS)))
        b_router = jnp.pad(jnp.concatenate([b_group[i], b_expert[i]]), (0, LANES - N_GROUPS - N_EXPERTS))
        h1, route, counts = _route(o_sb, o_gdn, h, w_out[i], row2(ln1_g[i]), row2(ln1_b[i]),
                                   w_router, row2(b_router), tm=tm, l=l, alpha=alpha)

        ids = route[..., :4].astype(jnp.int32).reshape(tp // tm, tm, 4)
        cnt = counts[0, :N_EXPERTS].astype(jnp.int32)
        tiles = (cnt + EXPERT_TILE - 1) // EXPERT_TILE
        tile_end = jnp.cumsum(tiles)
        base = (tile_end - tiles) * EXPERT_TILE
        tile_expert = jnp.minimum(
            jnp.sum(jnp.arange(n_tiles, dtype=jnp.int32)[:, None] >= tile_end[None, :], axis=1),
            N_EXPERTS - 1).astype(jnp.int32)
        n_used = tile_end[-1:].astype(jnp.int32)
        is_expert = ids[..., 0:2, None] == jnp.arange(N_EXPERTS, dtype=jnp.int32)
        seg_base = jnp.sum(jnp.where(is_expert, base, 0), axis=-1)
        pos = (seg_base + ids[..., 2:4]).transpose(0, 2, 1).astype(jnp.int32)

        xs = _dispatch(pos, tile_end.astype(jnp.int32), h1.reshape(tp, d), n_tiles * EXPERT_TILE, tm=tm)
        ys = _experts(tile_expert, n_used, xs, w1, w3, w2, i)
        h = _combine(pos, route, h1, row2(ln2_g[i]), row2(ln2_b[i]), ys, tm=tm, l=l, alpha=alpha)

    return h[:, FRONT + N_META:FRONT + l]
```
